```python
import jax, jax.numpy as jnp
from jax import lax
import numpy as np

D_MODEL = 1024
BATCH = 8
SEQ = 2048
DEPTH = 4

N_MIXERS = 3
N_LAYERS_A = (DEPTH + 2) // 3
N_LAYERS_B = (DEPTH + 1) // 3
N_LAYERS_C = DEPTH // 3

DILATED_CFG = ((128, 1), (512, 4), (2048, 16))
N_GROUPS_A = len(DILATED_CFG)
HEADS_PER_GROUP = 8
HEAD_DIM = 64
N_HEADS_A = N_GROUPS_A * HEADS_PER_GROUP
ATTN_WIDTH = HEADS_PER_GROUP * HEAD_DIM
QKV_WIDTH = N_GROUPS_A * 3 * ATTN_WIDTH
BLOCK = 128
NEG_INF = -1e30

SHORT_CONV_WIDTH = 3

POOL_WINDOWS = (2, 4, 8, 16)
N_POOL_GROUPS = len(POOL_WINDOWS)
POOL_GROUP_DIM = D_MODEL // N_POOL_GROUPS

D_FF = 2816
FFN_CONV_WIDTH = 3

RMS_EPS = 1e-6

kernel_name = "interleaved_dilated_attn_shortconv_pool_convffn"


def rms_norm(x, g):
    xf = x.astype(jnp.float32)
    y = xf * lax.rsqrt(jnp.mean(xf * xf, axis=-1, keepdims=True) + RMS_EPS)
    return (y * g.astype(jnp.float32)).astype(x.dtype)


def causal_depthwise_conv(x, w):
    k_w, c = w.shape
    return lax.conv_general_dilated(
        x, w.astype(x.dtype)[:, None, :], window_strides=(1,), padding=[(k_w - 1, 0)],
        dimension_numbers=("NWC", "WIO", "NWC"), feature_group_count=c)


def alibi_slopes():
    return jnp.asarray(2.0 ** (-8.0 * np.arange(1, N_HEADS_A + 1) / N_HEADS_A), jnp.float32)


def dilated_group_attention(q, k, v, slopes, window, dilation):
    b, s, h, dh = q.shape
    span = window // dilation
    length = s // dilation
    nb = -(-length // BLOCK)
    lp = nb * BLOCK

    def residues(t):
        return t.reshape(b, length, dilation, h, dh).transpose(0, 2, 3, 1, 4)

    qb = jnp.pad(residues(q), ((0, 0), (0, 0), (0, 0), (0, lp - length), (0, 0)))
    qb = qb.reshape(b, dilation, h, nb, BLOCK, dh)

    def band(t):
        tb = jnp.pad(residues(t), ((0, 0), (0, 0), (0, 0), (BLOCK, lp - length), (0, 0)))
        tb = tb.reshape(b, dilation, h, nb + 1, BLOCK, dh)
        return jnp.concatenate([tb[:, :, :, :-1], tb[:, :, :, 1:]], axis=4)

    kb, vb = band(k), band(v)
    scores = jnp.einsum("brhnqc,brhnkc->brhnqk", qb, kb).astype(jnp.float32) * (HEAD_DIM ** -0.5)

    delta = BLOCK + np.arange(BLOCK)[:, None] - np.arange(2 * BLOCK)[None, :]
    valid = (delta >= 0) & (delta <= span)
    valid = valid[None] & ((np.arange(nb)[:, None, None] > 0) | (np.arange(2 * BLOCK)[None, None, :] >= BLOCK))
    bias = -slopes[:, None, None] * jnp.asarray(delta * dilation, jnp.float32)[None]
    scores = jnp.where(valid, scores + bias[:, None], NEG_INF)

    lse = jax.nn.logsumexp(scores, axis=-1)
    p = jnp.exp(scores - lse[..., None])
    o = jnp.einsum("brhnqk,brhnkc->brhnqc", p.astype(vb.dtype), vb)
    o = o.reshape(b, dilation, h, lp, dh)[:, :, :, :length].transpose(0, 3, 1, 2, 4).reshape(b, s, h, dh)
    lse = lse.reshape(b, dilation, h, lp)[..., :length].transpose(0, 3, 1, 2).reshape(b, s, h)
    return o, lse


def dilated_attention_mixer(x, w_qkv, w_o):
    b, s, _ = x.shape
    qkv = (x @ w_qkv).reshape(b, s, N_GROUPS_A, 3, HEADS_PER_GROUP, HEAD_DIM)
    slopes = alibi_slopes()
    outs, lses = [], []
    for g, (window, dilation) in enumerate(DILATED_CFG):
        o, lse = dilated_group_attention(
            qkv[:, :, g, 0], qkv[:, :, g, 1], qkv[:, :, g, 2],
            slopes[g * HEADS_PER_GROUP:(g + 1) * HEADS_PER_GROUP], window, dilation)
        outs.append(o)
        lses.append(lse)
    wts = jax.nn.softmax(jnp.stack(lses, 0), axis=0)
    merged = jnp.einsum("gbsh,gbshc->bshc", wts, jnp.stack(outs, 0).astype(jnp.float32))
    return merged.reshape(b, s, ATTN_WIDTH).astype(x.dtype) @ w_o


def short_conv_mixer(x, w_in, w_dw, w_out):
    gate_b, gate_c, h = jnp.split(x @ w_in, 3, axis=-1)
    return (gate_b * causal_depthwise_conv(gate_c * h, w_dw)) @ w_out


def pooling_mixer(x, w_in, w_grp, scale, w_out):
    b, s, _ = x.shape
    u = (x @ w_in).astype(jnp.float32).reshape(b, s, N_POOL_GROUPS, POOL_GROUP_DIM)
    cs = jnp.cumsum(u, axis=1)
    pos = jnp.arange(1, s + 1, dtype=jnp.float32)[None, :, None]
    pooled = []
    for g, w in enumerate(POOL_WINDOWS):
        c = cs[:, :, g]
        prev = jnp.pad(c, ((0, 0), (w, 0), (0, 0)))[:, :s]
        pooled.append((c - prev) / jnp.minimum(pos, float(w)) - u[:, :, g])
    p = jnp.stack(pooled, axis=2).astype(x.dtype)
    y = jnp.einsum("bsgc,gcd->bsgd", p, w_grp).reshape(b, s, D_MODEL) * scale
    return y @ w_out


def conv_ffn(x, w_up, w_dw, w_down):
    h = causal_depthwise_conv(x @ w_up, w_dw)
    g, u = jnp.split(h, 2, axis=-1)
    return (jax.nn.silu(g) * u) @ w_down


def setup_inputs(seed: int = 0) -> dict:
    key = jax.random.key(seed)
    ks = jax.random.split(key, 16)
    f32 = jnp.float32

    def nrm(k, shape, fan_in):
        return jax.random.normal(k, shape, f32) * (fan_in ** -0.5)

    return {
        "x": jax.random.normal(ks[0], (BATCH, SEQ, D_MODEL), f32),
        "norm_g": 1.0 + 0.05 * jax.random.normal(ks[1], (DEPTH, 4, D_MODEL), f32),
        "attn_w_qkv": nrm(ks[2], (N_LAYERS_A, D_MODEL, QKV_WIDTH), D_MODEL),
        "attn_w_o": nrm(ks[3], (N_LAYERS_A, ATTN_WIDTH, D_MODEL), ATTN_WIDTH),
        "conv_w_in": nrm(ks[4], (N_LAYERS_B, D_MODEL, 3 * D_MODEL), D_MODEL),
        "conv_w_dw": nrm(ks[5], (N_LAYERS_B, SHORT_CONV_WIDTH, D_MODEL), SHORT_CONV_WIDTH),
        "conv_w_out": nrm(ks[6], (N_LAYERS_B, D_MODEL, D_MODEL), D_MODEL),
        "pool_w_in": nrm(ks[7], (N_LAYERS_C, D_MODEL, D_MODEL), D_MODEL),
        "pool_w_grp": nrm(ks[8], (N_LAYERS_C, N_POOL_GROUPS, POOL_GROUP_DIM, POOL_GROUP_DIM), POOL_GROUP_DIM),
        "pool_scale": 1.0 + 0.1 * jax.random.normal(ks[9], (N_LAYERS_C, D_MODEL), f32),
        "pool_w_out": nrm(ks[10], (N_LAYERS_C, D_MODEL, D_MODEL), D_MODEL),
        "ffn_w_up": nrm(ks[11], (DEPTH, D_MODEL, 2 * D_FF), D_MODEL),
        "ffn_w_dw": nrm(ks[12], (DEPTH, FFN_CONV_WIDTH, 2 * D_FF), FFN_CONV_WIDTH),
        "ffn_w_down": nrm(ks[13], (DEPTH, D_FF, D_MODEL), D_FF),
    }


def reference(x, norm_g, attn_w_qkv, attn_w_o, conv_w_in, conv_w_dw, conv_w_out,
              pool_w_in, pool_w_grp, pool_scale, pool_w_out, ffn_w_up, ffn_w_dw, ffn_w_down):
    ia = ib = ic = 0
    for i in range(DEPTH):
        h = rms_norm(x, norm_g[i, 0])
        kind = i % N_MIXERS
        if kind == 0:
            h = dilated_attention_mixer(h, attn_w_qkv[ia], attn_w_o[ia])
            ia += 1
        elif kind == 1:
            h = short_conv_mixer(h, conv_w_in[ib], conv_w_dw[ib], conv_w_out[ib])
            ib += 1
        else:
            h = pooling_mixer(h, pool_w_in[ic], pool_w_grp[ic], pool_scale[ic], pool_w_out[ic])
            ic += 1
        x = x + rms_norm(h, norm_g[i, 1])
        h = conv_ffn(rms_norm(x, norm_g[i, 2]), ffn_w_up[i], ffn_w_dw[i], ffn_w_down[i])
        x = x + rms_norm(h, norm_g[i, 3])
    return x
```

```python
import functools

import numpy as np
import jax
import jax.numpy as jnp
from jax import lax
from jax.experimental import pallas as pl
from jax.experimental.pallas import tpu as pltpu

F32 = jnp.float32
BF16 = jnp.bfloat16

RMS_EPS = 1e-6
NEG_INF = -1e30

DILATED_CFG = ((128, 1), (512, 4), (2048, 16))
HEADS_PER_GROUP = 8
HEAD_DIM = 64
ATTN_BLOCK = 128
POOL_WINDOWS = (2, 4, 8, 16)

LANES = 128
MXU_DIM = 256
VMEM_CAP_BYTES = 60000 * 1024

ROW_TILE = 512
CONV_HALO = 8
POOL_HALO = 16


def _vmem_limit(nbytes):
    return int(min(VMEM_CAP_BYTES, nbytes + (16 << 20)))


def _rms(x, g):
    ms = jnp.mean(x * x, axis=-1, keepdims=True)
    return x * lax.rsqrt(ms + RMS_EPS) * g


def _aligned(v, m):
    return v if isinstance(v, int) else pl.multiple_of(v, m)


def _resident(shape):
    nd = len(shape)
    return pl.BlockSpec(shape, lambda *_: (0,) * nd, pipeline_mode=pl.Buffered(1))


def _causal_conv3(cur, halo_ref, carry_ref, w, first, rows):
    halo_ref[0:CONV_HALO, :] = jnp.where(first, 0.0, carry_ref[...])
    halo_ref[CONV_HALO:CONV_HALO + rows, :] = cur
    carry_ref[...] = cur[rows - CONV_HALO:rows, :]
    return (cur * w[2:3, :]
            + halo_ref[CONV_HALO - 1:CONV_HALO - 1 + rows, :] * w[1:2, :]
            + halo_ref[CONV_HALO - 2:CONV_HALO - 2 + rows, :] * w[0:1, :])


def _ffn_kernel(x_ref, gin_ref, wup_ref, wdw_ref, wdown_ref, gout_ref, o_ref,
                halo_ref, carry_ref, act_ref, *, rows, n_chunks, chunk):
    first = pl.program_id(1) == 0
    x = x_ref[0]
    xn = _rms(x, gin_ref[...]).astype(BF16)
    for c in range(n_chunks):
        h = jnp.dot(xn, wup_ref[c], preferred_element_type=F32)
        y = _causal_conv3(h, halo_ref.at[c % 2], carry_ref.at[c], wdw_ref[c], first, rows)
        gate = y[:, :chunk]
        act = gate * jax.nn.sigmoid(gate) * y[:, chunk:]
        act_ref[:, c * chunk:(c + 1) * chunk] = act.astype(BF16)
    out = jnp.dot(act_ref[...], wdown_ref[...], preferred_element_type=F32)
    o_ref[0] = x + _rms(out, gout_ref[...])


def _ffn_layer(x, g_in, g_out, w_up, w_dw, w_down):
    b, s, d = x.shape
    d_ff = w_down.shape[0]
    chunk = MXU_DIM
    n_chunks = d_ff // chunk
    assert n_chunks * chunk == d_ff and s % ROW_TILE == 0
    rows = ROW_TILE
    wup = jnp.concatenate(
        [w_up[:, :d_ff].reshape(d, n_chunks, chunk), w_up[:, d_ff:].reshape(d, n_chunks, chunk)],
        axis=-1).transpose(1, 0, 2).astype(BF16)
    wdw = jnp.concatenate(
        [w_dw[:, :d_ff].reshape(-1, n_chunks, chunk), w_dw[:, d_ff:].reshape(-1, n_chunks, chunk)],
        axis=-1).transpose(1, 0, 2)
    wdown = w_down.astype(BF16)
    nbytes = (wup.size * 2 + wdown.size * 2 + wdw.size * 4 * 4
              + 4 * rows * d * 4
              + 2 * (CONV_HALO + rows) * 2 * chunk * 4 + n_chunks * CONV_HALO * 2 * chunk * 4
              + rows * d_ff * 2)
    return pl.pallas_call(
        functools.partial(_ffn_kernel, rows=rows, n_chunks=n_chunks, chunk=chunk),
        out_shape=jax.ShapeDtypeStruct(x.shape, F32),
        grid=(b, s // rows),
        in_specs=[
            pl.BlockSpec((1, rows, d), lambda bi, i: (bi, i, 0)),
            _resident((1, d)),
            _resident(wup.shape),
            _resident(wdw.shape),
            _resident(wdown.shape),
            _resident((1, d)),
        ],
        out_specs=pl.BlockSpec((1, rows, d), lambda bi, i: (bi, i, 0)),
        scratch_shapes=[
            pltpu.VMEM((2, CONV_HALO + rows, 2 * chunk), F32),
            pltpu.VMEM((n_chunks, CONV_HALO, 2 * chunk), F32),
            pltpu.VMEM((rows, d_ff), BF16),
        ],
        compiler_params=pltpu.CompilerParams(
            dimension_semantics=("arbitrary", "arbitrary"),
            vmem_limit_bytes=_vmem_limit(nbytes)),
        name="conv_ffn",
    )(x, g_in.reshape(1, d), wup, wdw, wdown, g_out.reshape(1, d))


def _conv_mixer_kernel(x_ref, gin_ref, win_ref, wdw_ref, wout_ref, gout_ref, o_ref,
                       halo_ref, carry_ref, y_ref, *, rows, n_chunks, chunk):
    first = pl.program_id(1) == 0
    x = x_ref[0]
    xn = _rms(x, gin_ref[...]).astype(BF16)
    for c in range(n_chunks):
        bch = jnp.dot(xn, win_ref[c], preferred_element_type=F32)
        z = bch[:, chunk:2 * chunk] * bch[:, 2 * chunk:]
        conv = _causal_conv3(z, halo_ref.at[c % 2], carry_ref.at[c], wdw_ref[c], first, rows)
        y_ref[:, c * chunk:(c + 1) * chunk] = (bch[:, :chunk] * conv).astype(BF16)
    out = jnp.dot(y_ref[...], wout_ref[...], preferred_element_type=F32)
    o_ref[0] = x + _rms(out, gout_ref[...])


def _conv_mixer_layer(x, g_in, g_out, w_in, w_dw, w_out):
    b, s, d = x.shape
    chunk = MXU_DIM
    n_chunks = d // chunk
    rows = ROW_TILE
    win = w_in.reshape(d, 3, n_chunks, chunk).transpose(2, 0, 1, 3).reshape(n_chunks, d, 3 * chunk)
    win = win.astype(BF16)
    wdw = w_dw.reshape(-1, n_chunks, chunk).transpose(1, 0, 2)
    wout = w_out.astype(BF16)
    nbytes = (win.size * 2 + wout.size * 2 + wdw.size * 4 * 4 + 4 * rows * d * 4
              + 2 * (CONV_HALO + rows) * chunk * 4 + n_chunks * CONV_HALO * chunk * 4
              + rows * d * 2)
    return pl.pallas_call(
        functools.partial(_conv_mixer_kernel, rows=rows, n_chunks=n_chunks, chunk=chunk),
        out_shape=jax.ShapeDtypeStruct(x.shape, F32),
        grid=(b, s // rows),
        in_specs=[
            pl.BlockSpec((1, rows, d), lambda bi, i: (bi, i, 0)),
            _resident((1, d)),
            _resident(win.shape),
            _resident(wdw.shape),
            _resident(wout.shape),
            _resident((1, d)),
        ],
        out_specs=pl.BlockSpec((1, rows, d), lambda bi, i: (bi, i, 0)),
        scratch_shapes=[
            pltpu.VMEM((2, CONV_HALO + rows, chunk), F32),
            pltpu.VMEM((n_chunks, CONV_HALO, chunk), F32),
            pltpu.VMEM((rows, d), BF16),
        ],
        compiler_params=pltpu.CompilerParams(
            dimension_semantics=("arbitrary", "arbitrary"),
            vmem_limit_bytes=_vmem_limit(nbytes)),
        name="short_conv_mixer",
    )(x, g_in.reshape(1, d), win, wdw, wout, g_out.reshape(1, d))


def _pool_mixer_kernel(x_ref, gin_ref, win_ref, wgrp_ref, scale_ref, wout_ref, gout_ref, o_ref,
                       halo_ref, carry_ref, y_ref, *, rows, group_dim):
    i = pl.program_id(1)
    first = i == 0
    x = x_ref[0]
    xn = _rms(x, gin_ref[...]).astype(BF16)
    u = jnp.dot(xn, win_ref[...], preferred_element_type=F32)
    halo_ref[0:POOL_HALO, :] = jnp.where(first, 0.0, carry_ref[...])
    halo_ref[POOL_HALO:POOL_HALO + rows, :] = u
    carry_ref[...] = u[rows - POOL_HALO:rows, :]
    pos = i * rows + 1 + lax.broadcasted_iota(jnp.int32, (rows, group_dim), 0)
    for g, w in enumerate(POOL_WINDOWS):
        cols = slice(g * group_dim, (g + 1) * group_dim)
        ug = u[:, cols]
        total = ug
        for k in range(1, w):
            total = total + halo_ref[POOL_HALO - k:POOL_HALO - k + rows, cols]
        count = jnp.minimum(pos, w).astype(F32)
        pooled = total / count - ug
        yg = jnp.dot(pooled.astype(BF16), wgrp_ref[g], preferred_element_type=F32)
        y_ref[:, cols] = (yg * scale_ref[:, cols]).astype(BF16)
    out = jnp.dot(y_ref[...], wout_ref[...], preferred_element_type=F32)
    o_ref[0] = x + _rms(out, gout_ref[...])


def _pool_mixer_layer(x, g_in, g_out, w_in, w_grp, scale, w_out):
    b, s, d = x.shape
    n_groups, group_dim, _ = w_grp.shape
    assert n_groups == len(POOL_WINDOWS) and n_groups * group_dim == d
    rows = ROW_TILE
    win, wgrp, wout = w_in.astype(BF16), w_grp.astype(BF16), w_out.astype(BF16)
    nbytes = (win.size * 2 + wgrp.size * 2 + wout.size * 2 + 4 * rows * d * 4
              + (POOL_HALO + rows) * d * 4 + POOL_HALO * d * 4 + rows * d * 2)
    return pl.pallas_call(
        functools.partial(_pool_mixer_kernel, rows=rows, group_dim=group_dim),
        out_shape=jax.ShapeDtypeStruct(x.shape, F32),
        grid=(b, s // rows),
        in_specs=[
            pl.BlockSpec((1, rows, d), lambda bi, i: (bi, i, 0)),
            _resident((1, d)),
            _resident(win.shape),
            _resident(wgrp.shape),
            _resident((1, d)),
            _resident(wout.shape),
            _resident((1, d)),
        ],
        out_specs=pl.BlockSpec((1, rows, d), lambda bi, i: (bi, i, 0)),
        scratch_shapes=[
            pltpu.VMEM((POOL_HALO + rows, d), F32),
            pltpu.VMEM((POOL_HALO, d), F32),
            pltpu.VMEM((rows, d), BF16),
        ],
        compiler_params=pltpu.CompilerParams(
            dimension_semantics=("arbitrary", "arbitrary"),
            vmem_limit_bytes=_vmem_limit(nbytes)),
        name="pool_mixer",
    )(x, g_in.reshape(1, d), win, wgrp, scale.reshape(1, d), wout, g_out.reshape(1, d))


def _qkv_kernel(x_ref, gin_ref, w_ref, o_ref, *, n_sub, d, n_pairs):
    parts = [_rms(x_ref[0, :, k * d:(k + 1) * d], gin_ref[...]).astype(BF16) for k in range(n_sub)]
    xn = parts[0] if n_sub == 1 else jnp.concatenate(parts, axis=0)
    y = jnp.dot(xn, w_ref[...], preferred_element_type=F32)
    width = n_pairs * LANES
    for i in range(3):
        for hp in range(n_pairs):
            lo = i * width + hp * LANES
            o_ref[0, i, hp] = y[:, lo:lo + LANES].astype(BF16)


def _qkv_group(x, g_in, w_group, dilation):
    b, s, d = x.shape
    length = s // dilation
    rows = ROW_TILE
    n_pairs = w_group.shape[1] // (3 * LANES)
    n_sub = max(1, rows // length)
    sub_rows = rows // n_sub
    xv = x.reshape(b, length, dilation * d)
    tiles_per_res = length // sub_rows

    def x_index(bi, j):
        if n_sub > 1:
            return (bi, 0, j)
        return (bi, j % tiles_per_res, j // tiles_per_res)

    nbytes = w_group.size * 2 + 2 * rows * d * 4 + 2 * rows * 3 * n_pairs * LANES * 2 + rows * 3 * n_pairs * LANES * 4
    return pl.pallas_call(
        functools.partial(_qkv_kernel, n_sub=n_sub, d=d, n_pairs=n_pairs),
        out_shape=jax.ShapeDtypeStruct((b, 3, n_pairs, s, LANES), BF16),
        grid=(b, s // rows),
        in_specs=[
            pl.BlockSpec((1, sub_rows, n_sub * d), x_index),
            _resident((1, d)),
            _resident(w_group.shape),
        ],
        out_specs=pl.BlockSpec((1, 3, n_pairs, rows, LANES), lambda bi, j: (bi, 0, 0, j, 0)),
        compiler_params=pltpu.CompilerParams(
            dimension_semantics=("arbitrary", "arbitrary"),
            vmem_limit_bytes=_vmem_limit(nbytes)),
        name=f"qkv_dil{dilation}",
    )(xv, g_in.reshape(1, d), w_group)


def _attn_bias_table(n_pairs):
    n_heads = len(DILATED_CFG) * HEADS_PER_GROUP
    slopes = (2.0 ** (-8.0 * np.arange(1, n_heads + 1) / n_heads)).astype(np.float32)
    delta = ATTN_BLOCK + np.arange(ATTN_BLOCK)[:, None] - np.arange(2 * ATTN_BLOCK)[None, :]
    tables = []
    for g, (window, dilation) in enumerate(DILATED_CFG):
        span = window // dilation
        valid = (delta >= 0) & (delta <= span)
        dist = (delta * dilation).astype(np.float32)
        per_head = [np.where(valid, -slopes[g * HEADS_PER_GROUP + h] * dist, np.float32(NEG_INF))
                    for h in range(HEADS_PER_GROUP)]
        tables.append(np.stack(per_head).reshape(n_pairs, 2, ATTN_BLOCK, 2 * ATTN_BLOCK))
    return jnp.asarray(np.stack(tables), F32)


def _attn_kernel(q0, k0, v0, q1, k1, v1, q2, k2, v2, bias_ref, o_ref, acc_ref, m_ref, l_ref, *, seq):
    blk = ATTN_BLOCK
    low_half = lax.broadcasted_iota(jnp.int32, (blk, LANES), 1) < HEAD_DIM
    scale = HEAD_DIM ** -0.5

    def block(g, q_ref, k_ref, v_ref, q_start, with_prev, out_rows, first_group):
        q_start = _aligned(q_start, blk)
        q = q_ref[pl.ds(q_start, blk), :].astype(F32) * scale
        if with_prev:
            k_start, n_keys, col0 = _aligned(q_start - blk, blk), 2 * blk, 0
        else:
            k_start, n_keys, col0 = q_start, blk, blk
        k = k_ref[pl.ds(k_start, n_keys), :]
        v = v_ref[pl.ds(k_start, n_keys), :]
        m_h, l_h, pv_h = [], [], []
        for a in range(2):
            qa = jnp.where(low_half if a == 0 else ~low_half, q, 0.0).astype(BF16)
            s = lax.dot_general(qa, k, (((1,), (1,)), ((), ())), preferred_element_type=F32)
            s = s + bias_ref[g, a, :, col0:col0 + n_keys]
            m = jnp.max(s, axis=-1, keepdims=True)
            p = jnp.exp(s - m)
            m_h.append(m)
            l_h.append(jnp.sum(p, axis=-1, keepdims=True))
            pv_h.append(jnp.dot(p.astype(BF16), v, preferred_element_type=F32))
        m_b = jnp.where(low_half, m_h[0], m_h[1])
        l_b = jnp.where(low_half, l_h[0], l_h[1])
        pv_b = jnp.where(low_half, pv_h[0], pv_h[1])
        if first_group:
            m_ref[out_rows, :] = m_b
            l_ref[out_rows, :] = l_b
            acc_ref[out_rows, :] = pv_b
        else:
            m_old = m_ref[out_rows, :]
            m_new = jnp.maximum(m_old, m_b)
            a_old = jnp.exp(m_old - m_new)
            a_blk = jnp.exp(m_b - m_new)
            m_ref[out_rows, :] = m_new
            l_ref[out_rows, :] = l_ref[out_rows, :] * a_old + l_b * a_blk
            acc_ref[out_rows, :] = acc_ref[out_rows, :] * a_old + pv_b * a_blk

    groups = ((q0, k0, v0), (q1, k1, v1), (q2, k2, v2))
    for g, ((_, dilation), (q_ref, k_ref, v_ref)) in enumerate(zip(DILATED_CFG, groups)):
        length = seq // dilation
        n_blocks = length // blk

        def residue(r, carry, g=g, dilation=dilation, n_blocks=n_blocks, refs=(q_ref, k_ref, v_ref)):
            def rows_of(n):
                start = n * (blk * dilation) + r
                if dilation == 1:
                    return pl.ds(_aligned(start, blk), blk)
                return pl.ds(start, blk, stride=dilation)

            base = r * n_blocks
            block(g, *refs, base * blk, False, rows_of(0), g == 0)

            def inner(n, c):
                block(g, *refs, (base + n) * blk, True, rows_of(n), g == 0)
                return c

            if n_blocks > 1:
                lax.fori_loop(1, n_blocks, inner, 0)
            return carry

        if dilation == 1:
            residue(0, 0)
        else:
            lax.fori_loop(0, dilation, residue, 0)

    o_ref[0] = (acc_ref[...] / l_ref[...]).astype(BF16)


def _attn_core(qkv_groups, bias):
    b, _, n_pairs, s, _ = qkv_groups[0].shape
    in_specs, args = [], []
    for qkv in qkv_groups:
        for i in range(3):
            in_specs.append(pl.BlockSpec((None, None, None, s, LANES),
                                         lambda bi, hp, i=i: (bi, i, hp, 0, 0)))
            args.append(qkv)
    n_groups = bias.shape[0]
    in_specs.append(pl.BlockSpec((n_groups, None, 2, ATTN_BLOCK, 2 * ATTN_BLOCK),
                                 lambda bi, hp: (0, hp, 0, 0, 0)))
    nbytes = (2 * 9 * s * LANES * 2 + 2 * n_groups * 2 * ATTN_BLOCK * 2 * ATTN_BLOCK * 4
              + 2 * s * LANES * 2 + 3 * s * LANES * 4)
    return pl.pallas_call(
        functools.partial(_attn_kernel, seq=s),
        out_shape=jax.ShapeDtypeStruct((b, s, n_pairs * LANES), BF16),
        grid=(b, n_pairs),
        in_specs=in_specs,
        out_specs=pl.BlockSpec((1, s, LANES), lambda bi, hp: (bi, 0, hp)),
        scratch_shapes=[pltpu.VMEM((s, LANES), F32)] * 3,
        compiler_params=pltpu.CompilerParams(
            dimension_semantics=("arbitrary", "arbitrary"),
            vmem_limit_bytes=_vmem_limit(nbytes)),
        name="dilated_attn",
    )(*args, bias)


def _proj_res_kernel(y_ref, w_ref, gout_ref, x_ref, o_ref):
    out = jnp.dot(y_ref[0], w_ref[...], preferred_element_type=F32)
    o_ref[0] = x_ref[0] + _rms(out, gout_ref[...])


def _proj_res(y, w, g_out, x):
    b, s, d = x.shape
    rows = ROW_TILE
    kdim = y.shape[-1]
    nbytes = w.size * 2 + 2 * rows * kdim * 2 + 4 * rows * d * 4
    return pl.pallas_call(
        _proj_res_kernel,
        out_shape=jax.ShapeDtypeStruct(x.shape, F32),
        grid=(b, s // rows),
        in_specs=[
            pl.BlockSpec((1, rows, kdim), lambda bi, i: (bi, i, 0)),
            _resident(w.shape),
            _resident((1, d)),
            pl.BlockSpec((1, rows, d), lambda bi, i: (bi, i, 0)),
        ],
        out_specs=pl.BlockSpec((1, rows, d), lambda bi, i: (bi, i, 0)),
        compiler_params=pltpu.CompilerParams(
            dimension_semantics=("arbitrary", "arbitrary"),
            vmem_limit_bytes=_vmem_limit(nbytes)),
        name="attn_out_proj",
    )(y, w, g_out.reshape(1, d), x)


def _attn_layer(x, g_in, g_out, w_qkv, w_o):
    d = x.shape[-1]
    n_groups = len(DILATED_CFG)
    group_width = w_qkv.shape[1] // n_groups
    n_pairs = group_width // (3 * LANES)
    assert n_pairs * 2 == HEADS_PER_GROUP and 2 * HEAD_DIM == LANES
    w = w_qkv.astype(BF16)
    qkv_groups = [
        _qkv_group(x, g_in, w[:, g * group_width:(g + 1) * group_width], dilation)
        for g, (_, dilation) in enumerate(DILATED_CFG)
    ]
    merged = _attn_core(qkv_groups, _attn_bias_table(n_pairs))
    return _proj_res(merged, w_o.astype(BF16), g_out, x)


def kernel(x, norm_g, attn_w_qkv, attn_w_o, conv_w_in, conv_w_dw, conv_w_out,
           pool_w_in, pool_w_grp, pool_scale, pool_w_out, ffn_w_up, ffn_w_dw, ffn_w_down):
    depth = norm_g.shape[0]
    ia = ib = ic = 0
    for i in range(depth):
        g = norm_g[i]
        kind = i % 3
        if kind == 0:
            x = _attn_layer(x, g[0], g[1], attn_w_qkv[ia], attn_w_o[ia])
            ia += 1
        elif kind == 1:
            x = _conv_mixer_layer(x, g[0], g[1], conv_w_in[ib], conv_w_dw[ib], conv_w_out[ib])
            ib += 1
        else:
            x = _pool_mixer_layer(x, g[0], g[1], pool_w_in[ic], pool_w_grp[ic], pool_scale[ic],
                                  pool_w_out[ic])
            ic += 1
        x = _ffn_layer(x, g[2], g[3], ffn_w_up[i], ffn_w_dw[i], ffn_w_down[i])
    return x
```

```python
import functools

import numpy as np
import jax
import jax.numpy as jnp
from jax import lax
from jax.experimental import pallas as pl
from jax.experimental.pallas import tpu as pltpu

F32 = jnp.float32
BF16 = jnp.bfloat16

RMS_EPS = 1e-6
NEG_INF = -1e30

DILATED_CFG = ((128, 1), (512, 4), (2048, 16))
HEADS_PER_GROUP = 8
HEAD_DIM = 64
ATTN_BLOCK = 128
ATTN_UNROLL = 4
POOL_WINDOWS = (2, 4, 8, 16)

LANES = 128
MXU_DIM = 256
VMEM_CAP_BYTES = 60000 * 1024

ROW_TILE = 512
CONV_HALO = 8
POOL_HALO = 16


def _vmem_limit(nbytes):
    return int(min(VMEM_CAP_BYTES, nbytes + (16 << 20)))


def _rms(x, g):
    ms = jnp.mean(x * x, axis=-1, keepdims=True)
    return x * lax.rsqrt(ms + RMS_EPS) * g


def _aligned(v, m):
    return v if isinstance(v, int) else pl.multiple_of(v, m)


def _resident(shape):
    nd = len(shape)
    return pl.BlockSpec(shape, lambda *_: (0,) * nd, pipeline_mode=pl.Buffered(1))


def _row_tiled(rows, width):
    return pl.BlockSpec((1, rows, width), lambda bi, i: (bi, i, 0))


def _causal_conv3(cur, halo_ref, carry_ref, w, rows):
    halo_ref[0:CONV_HALO, :] = carry_ref[...]
    halo_ref[CONV_HALO:CONV_HALO + rows, :] = cur
    carry_ref[...] = cur[rows - CONV_HALO:rows, :]
    return (cur * w[2:3, :]
            + halo_ref[CONV_HALO - 1:CONV_HALO - 1 + rows, :] * w[1:2, :]
            + halo_ref[CONV_HALO - 2:CONV_HALO - 2 + rows, :] * w[0:1, :])


def _zero_at_sequence_start(carry_ref):
    @pl.when(pl.program_id(1) == 0)
    def _():
        carry_ref[...] = jnp.zeros(carry_ref.shape, carry_ref.dtype)


def _ffn_kernel(x_ref, gin_ref, wup_ref, wdw_ref, wdown_ref, gout_ref, o_ref,
                halo_ref, carry_ref, act_ref, *, rows, d_ff, chunk):
    _zero_at_sequence_start(carry_ref)
    x = x_ref[0]
    xn = _rms(x, gin_ref[...]).astype(BF16)
    for c in range(d_ff // chunk):
        halves = []
        for part in range(2):
            cols = slice(part * d_ff + c * chunk, part * d_ff + (c + 1) * chunk)
            h = jnp.dot(xn, wup_ref[:, cols], preferred_element_type=F32)
            slot = 2 * c + part
            halves.append(_causal_conv3(h, halo_ref.at[slot % 4], carry_ref.at[slot],
                                        wdw_ref[:, cols], rows))
        gate, lin = halves
        act_ref[:, c * chunk:(c + 1) * chunk] = (gate * jax.nn.sigmoid(gate) * lin).astype(BF16)
    out = jnp.dot(act_ref[...], wdown_ref[...], preferred_element_type=F32)
    o_ref[0] = x + _rms(out, gout_ref[...])


def _ffn_layer(x, g_in, g_out, w_up, w_dw, w_down):
    b, s, d = x.shape
    d_ff = w_down.shape[0]
    chunk = MXU_DIM
    rows = ROW_TILE
    assert d_ff % chunk == 0 and s % rows == 0
    n_slots = 2 * (d_ff // chunk)
    nbytes = (w_up.size * 2 + w_down.size * 2 + 8 * w_dw.shape[1] * 4
              + 4 * rows * d * 4
              + 4 * (CONV_HALO + rows) * chunk * 4 + n_slots * CONV_HALO * chunk * 4
              + rows * d_ff * 2)
    return pl.pallas_call(
        functools.partial(_ffn_kernel, rows=rows, d_ff=d_ff, chunk=chunk),
        out_shape=jax.ShapeDtypeStruct(x.shape, F32),
        grid=(b, s // rows),
        in_specs=[
            _row_tiled(rows, d),
            _resident((1, d)),
            _resident(w_up.shape),
            _resident(w_dw.shape),
            _resident(w_down.shape),
            _resident((1, d)),
        ],
        out_specs=_row_tiled(rows, d),
        scratch_shapes=[
            pltpu.VMEM((4, CONV_HALO + rows, chunk), F32),
            pltpu.VMEM((n_slots, CONV_HALO, chunk), F32),
            pltpu.VMEM((rows, d_ff), BF16),
        ],
        compiler_params=pltpu.CompilerParams(
            dimension_semantics=("arbitrary", "arbitrary"),
            vmem_limit_bytes=_vmem_limit(nbytes)),
        name="conv_ffn",
    )(x, g_in.reshape(1, d), w_up.astype(BF16), w_dw, w_down.astype(BF16), g_out.reshape(1, d))


def _conv_mixer_kernel(x_ref, gin_ref, win_ref, wdw_ref, wout_ref, gout_ref, o_ref,
                       halo_ref, carry_ref, y_ref, *, rows, d, chunk):
    _zero_at_sequence_start(carry_ref)
    x = x_ref[0]
    xn = _rms(x, gin_ref[...]).astype(BF16)
    for c in range(d // chunk):
        cols = slice(c * chunk, (c + 1) * chunk)
        gate_b, gate_c, h = [
            jnp.dot(xn, win_ref[:, part * d + c * chunk:part * d + (c + 1) * chunk],
                    preferred_element_type=F32) for part in range(3)]
        conv = _causal_conv3(gate_c * h, halo_ref.at[c % 2], carry_ref.at[c], wdw_ref[:, cols], rows)
        y_ref[:, cols] = (gate_b * conv).astype(BF16)
    out = jnp.dot(y_ref[...], wout_ref[...], preferred_element_type=F32)
    o_ref[0] = x + _rms(out, gout_ref[...])


def _conv_mixer_layer(x, g_in, g_out, w_in, w_dw, w_out):
    b, s, d = x.shape
    chunk = MXU_DIM
    rows = ROW_TILE
    n_chunks = d // chunk
    nbytes = (w_in.size * 2 + w_out.size * 2 + 8 * d * 4 + 4 * rows * d * 4
              + 2 * (CONV_HALO + rows) * chunk * 4 + n_chunks * CONV_HALO * chunk * 4
              + rows * d * 2)
    return pl.pallas_call(
        functools.partial(_conv_mixer_kernel, rows=rows, d=d, chunk=chunk),
        out_shape=jax.ShapeDtypeStruct(x.shape, F32),
        grid=(b, s // rows),
        in_specs=[
            _row_tiled(rows, d),
            _resident((1, d)),
            _resident(w_in.shape),
            _resident(w_dw.shape),
            _resident(w_out.shape),
            _resident((1, d)),
        ],
        out_specs=_row_tiled(rows, d),
        scratch_shapes=[
            pltpu.VMEM((2, CONV_HALO + rows, chunk), F32),
            pltpu.VMEM((n_chunks, CONV_HALO, chunk), F32),
            pltpu.VMEM((rows, d), BF16),
        ],
        compiler_params=pltpu.CompilerParams(
            dimension_semantics=("arbitrary", "arbitrary"),
            vmem_limit_bytes=_vmem_limit(nbytes)),
        name="short_conv_mixer",
    )(x, g_in.reshape(1, d), w_in.astype(BF16), w_dw, w_out.astype(BF16), g_out.reshape(1, d))


def _pool_mixer_kernel(x_ref, gin_ref, win_ref, wgrp_ref, scale_ref, wout_ref, gout_ref, o_ref,
                       halo_ref, carry_ref, y_ref, *, rows, group_dim):
    _zero_at_sequence_start(carry_ref)
    x = x_ref[0]
    xn = _rms(x, gin_ref[...]).astype(BF16)
    u = jnp.dot(xn, win_ref[...], preferred_element_type=F32)
    halo_ref[0:POOL_HALO, :] = carry_ref[...]
    halo_ref[POOL_HALO:POOL_HALO + rows, :] = u
    carry_ref[...] = u[rows - POOL_HALO:rows, :]
    pos = pl.program_id(1) * rows + 1 + lax.broadcasted_iota(jnp.int32, (rows, group_dim), 0)
    for g, w in enumerate(POOL_WINDOWS):
        cols = slice(g * group_dim, (g + 1) * group_dim)
        ug = u[:, cols]
        total = ug
        for k in range(1, w):
            total = total + halo_ref[POOL_HALO - k:POOL_HALO - k + rows, cols]
        count = jnp.minimum(pos, w).astype(F32)
        pooled = total / count - ug
        yg = jnp.dot(pooled.astype(BF16), wgrp_ref[g], preferred_element_type=F32)
        y_ref[:, cols] = (yg * scale_ref[:, cols]).astype(BF16)
    out = jnp.dot(y_ref[...], wout_ref[...], preferred_element_type=F32)
    o_ref[0] = x + _rms(out, gout_ref[...])


def _pool_mixer_layer(x, g_in, g_out, w_in, w_grp, scale, w_out):
    b, s, d = x.shape
    n_groups, group_dim, _ = w_grp.shape
    assert n_groups == len(POOL_WINDOWS) and n_groups * group_dim == d
    rows = ROW_TILE
    nbytes = (w_in.size * 2 + w_grp.size * 2 + w_out.size * 2 + 4 * rows * d * 4
              + (POOL_HALO + rows) * d * 4 + POOL_HALO * d * 4 + rows * d * 2)
    return pl.pallas_call(
        functools.partial(_pool_mixer_kernel, rows=rows, group_dim=group_dim),
        out_shape=jax.ShapeDtypeStruct(x.shape, F32),
        grid=(b, s // rows),
        in_specs=[
            _row_tiled(rows, d),
            _resident((1, d)),
            _resident(w_in.shape),
            _resident(w_grp.shape),
            _resident((1, d)),
            _resident(w_out.shape),
            _resident((1, d)),
        ],
        out_specs=_row_tiled(rows, d),
        scratch_shapes=[
            pltpu.VMEM((POOL_HALO + rows, d), F32),
            pltpu.VMEM((POOL_HALO, d), F32),
            pltpu.VMEM((rows, d), BF16),
        ],
        compiler_params=pltpu.CompilerParams(
            dimension_semantics=("arbitrary", "arbitrary"),
            vmem_limit_bytes=_vmem_limit(nbytes)),
        name="pool_mixer",
    )(x, g_in.reshape(1, d), w_in.astype(BF16), w_grp.astype(BF16), scale.reshape(1, d),
      w_out.astype(BF16), g_out.reshape(1, d))


def _qkv_kernel(x_ref, gin_ref, w_ref, *refs, rows, d, n_pairs):
    n_groups = len(DILATED_CFG)
    outs, (slab_ref, perm_ref) = refs[:n_groups], refs[n_groups:]
    width = 3 * n_pairs * LANES
    xn = _rms(x_ref[0], gin_ref[...])
    n_slabs = d // LANES
    for s in range(n_slabs):
        slab_ref[s] = xn[:, s * LANES:(s + 1) * LANES]
    for g, (_, dilation) in enumerate(DILATED_CFG):
        sub = rows // dilation
        if dilation == 1:
            xg = xn.astype(BF16)
        else:
            perm = perm_ref.at[g % 2]
            for r in range(dilation):
                for s in range(n_slabs):
                    perm[r * sub:(r + 1) * sub, s * LANES:(s + 1) * LANES] = (
                        slab_ref[s, pl.ds(r, sub, stride=dilation), :].astype(BF16))
            xg = perm[...]
        y = jnp.dot(xg, w_ref[:, g * width:(g + 1) * width], preferred_element_type=F32)
        for i in range(3):
            for hp in range(n_pairs):
                lo = (i * n_pairs + hp) * LANES
                for r in range(dilation):
                    outs[g][0, i, hp, r] = y[r * sub:(r + 1) * sub, lo:lo + LANES].astype(BF16)


def _qkv_proj(x, g_in, w_qkv, n_pairs):
    b, s, d = x.shape
    rows = ROW_TILE
    out_shapes, out_specs = [], []
    for _, dilation in DILATED_CFG:
        length = s // dilation
        out_shapes.append(jax.ShapeDtypeStruct((b, 3, n_pairs, dilation, length, LANES), BF16))
        out_specs.append(pl.BlockSpec((1, 3, n_pairs, dilation, rows // dilation, LANES),
                                      lambda bi, j: (bi, 0, 0, 0, j, 0)))
    nbytes = (w_qkv.size * 2 + 2 * rows * d * 4 + 2 * rows * w_qkv.shape[1] * 2
              + rows * d * 4 + 2 * rows * d * 2)
    outs = pl.pallas_call(
        functools.partial(_qkv_kernel, rows=rows, d=d, n_pairs=n_pairs),
        out_shape=out_shapes,
        grid=(b, s // rows),
        in_specs=[_row_tiled(rows, d), _resident((1, d)), _resident(w_qkv.shape)],
        out_specs=out_specs,
        scratch_shapes=[
            pltpu.VMEM((d // LANES, rows, LANES), F32),
            pltpu.VMEM((2, rows, d), BF16),
        ],
        compiler_params=pltpu.CompilerParams(
            dimension_semantics=("arbitrary", "arbitrary"),
            vmem_limit_bytes=_vmem_limit(nbytes)),
        name="qkv_proj",
    )(x, g_in.reshape(1, d), w_qkv)
    return [o.reshape(b, 3, n_pairs, s, LANES) for o in outs]


def _attn_bias_table(n_pairs):
    n_heads = len(DILATED_CFG) * HEADS_PER_GROUP
    slopes = (2.0 ** (-8.0 * np.arange(1, n_heads + 1) / n_heads)).astype(np.float32)
    delta = ATTN_BLOCK + np.arange(ATTN_BLOCK)[:, None] - np.arange(2 * ATTN_BLOCK)[None, :]
    neg = np.float32(NEG_INF)
    tables = []
    for g, (window, dilation) in enumerate(DILATED_CFG):
        span = window // dilation
        valid = (delta >= 0) & (delta <= span)
        dist = (delta * dilation).astype(np.float32)
        with_prev = np.stack([np.where(valid, -slopes[g * HEADS_PER_GROUP + h] * dist, neg)
                              for h in range(HEADS_PER_GROUP)])
        with_prev = with_prev.reshape(n_pairs, 2, ATTN_BLOCK, 2 * ATTN_BLOCK)
        no_prev = with_prev.copy()
        no_prev[..., :ATTN_BLOCK] = neg
        row_zero = np.full_like(with_prev, neg)
        row_zero[..., :ATTN_BLOCK] = with_prev[..., ATTN_BLOCK:]
        tables += [with_prev, no_prev, row_zero]
    return jnp.asarray(np.stack(tables), F32)


WITH_PREV, NO_PREV, ROW_ZERO = 0, 1, 2
N_BIAS_KINDS = 3


def _attn_kernel(q0, k0, v0, q1, k1, v1, q2, k2, v2, bias_ref, o_ref, acc_ref, m_ref, l_ref, *, seq):
    blk = ATTN_BLOCK
    low_half = lax.broadcasted_iota(jnp.int32, (blk, LANES), 1) < HEAD_DIM
    scale = HEAD_DIM ** -0.5

    def scores(kind, q_ref, k_ref, v_ref, q_start, k_start):
        q = q_ref[pl.ds(_aligned(q_start, blk), blk), :].astype(F32) * scale
        k = k_ref[pl.ds(_aligned(k_start, blk), 2 * blk), :]
        v = v_ref[pl.ds(_aligned(k_start, blk), 2 * blk), :]
        v_ext = jnp.concatenate([v, jnp.ones((2 * blk, LANES), BF16)], axis=1)
        m_h, l_h, pv_h = [], [], []
        for a in range(2):
            qa = jnp.where(low_half if a == 0 else ~low_half, q, 0.0).astype(BF16)
            s = lax.dot_general(qa, k, (((1,), (1,)), ((), ())), preferred_element_type=F32)
            s = s + bias_ref[kind, a]
            m = jnp.max(s, axis=-1, keepdims=True)
            p = jnp.exp(s - m).astype(BF16)
            pvl = jnp.dot(p, v_ext, preferred_element_type=F32)
            m_h.append(m)
            pv_h.append(pvl[:, :LANES])
            l_h.append(pvl[:, LANES:])
        return (jnp.where(low_half, m_h[0], m_h[1]), jnp.where(low_half, l_h[0], l_h[1]),
                jnp.where(low_half, pv_h[0], pv_h[1]))

    def merge(out_rows, first_group, m_b, l_b, pv_b):
        if first_group:
            m_ref[out_rows, :] = m_b
            l_ref[out_rows, :] = l_b
            acc_ref[out_rows, :] = pv_b
        else:
            m_old = m_ref[out_rows, :]
            m_new = jnp.maximum(m_old, m_b)
            a_old = jnp.exp(m_old - m_new)
            a_blk = jnp.exp(m_b - m_new)
            m_ref[out_rows, :] = m_new
            l_ref[out_rows, :] = l_ref[out_rows, :] * a_old + l_b * a_blk
            acc_ref[out_rows, :] = acc_ref[out_rows, :] * a_old + pv_b * a_blk

    groups = ((q0, k0, v0), (q1, k1, v1), (q2, k2, v2))
    total_blocks = seq // blk
    order = sorted(range(len(DILATED_CFG)), key=lambda g: -DILATED_CFG[g][1])
    for g in order:
        dilation, refs, first_group = DILATED_CFG[g][1], groups[g], g == order[0]
        n_blocks = total_blocks // dilation
        assert n_blocks & (n_blocks - 1) == 0 and total_blocks % ATTN_UNROLL == 0

        def step(it, carry, g=g, dilation=dilation, refs=refs, first_group=first_group, n_blocks=n_blocks):
            work = []
            for j in range(ATTN_UNROLL):
                f = it * ATTN_UNROLL + j
                r = lax.shift_right_logical(f, n_blocks.bit_length() - 1)
                n = lax.bitwise_and(f, n_blocks - 1)
                kind = jnp.where(n == 0, jnp.where(f == 0, ROW_ZERO, NO_PREV), WITH_PREV)
                start = n * (blk * dilation) + r
                rows = (pl.ds(_aligned(start, blk), blk) if dilation == 1
                        else pl.ds(start, blk, stride=dilation))
                work.append((scores(g * N_BIAS_KINDS + kind, *refs, f * blk, jnp.maximum(f - 1, 0) * blk),
                             rows))
            for vals, rows in work:
                merge(rows, first_group, *vals)
            return carry

        lax.fori_loop(0, total_blocks // ATTN_UNROLL, step, 0)

    o_ref[0] = (acc_ref[...] / l_ref[...]).astype(BF16)


def _attn_core(qkv_groups, bias):
    b, _, n_pairs, s, _ = qkv_groups[0].shape
    in_specs, args = [], []
    for qkv in qkv_groups:
        for i in range(3):
            in_specs.append(pl.BlockSpec((None, None, None, s, LANES),
                                         lambda bi, hp, i=i: (bi, i, hp, 0, 0)))
            args.append(qkv)
    n_kinds = bias.shape[0]
    in_specs.append(pl.BlockSpec((n_kinds, None, 2, ATTN_BLOCK, 2 * ATTN_BLOCK),
                                 lambda bi, hp: (0, hp, 0, 0, 0)))
    nbytes = (2 * 9 * s * LANES * 2 + 2 * n_kinds * 2 * ATTN_BLOCK * 2 * ATTN_BLOCK * 4
              + 2 * s * LANES * 2 + 3 * s * LANES * 4)
    return pl.pallas_call(
        functools.partial(_attn_kernel, seq=s),
        out_shape=jax.ShapeDtypeStruct((b, s, n_pairs * LANES), BF16),
        grid=(b, n_pairs),
        in_specs=in_specs,
        out_specs=pl.BlockSpec((1, s, LANES), lambda bi, hp: (bi, 0, hp)),
        scratch_shapes=[pltpu.VMEM((s, LANES), F32)] * 3,
        compiler_params=pltpu.CompilerParams(
            dimension_semantics=("arbitrary", "arbitrary"),
            vmem_limit_bytes=_vmem_limit(nbytes)),
        name="dilated_attn",
    )(*args, bias)


def _proj_res_kernel(y_ref, w_ref, gout_ref, x_ref, o_ref):
    out = jnp.dot(y_ref[0], w_ref[...], preferred_element_type=F32)
    o_ref[0] = x_ref[0] + _rms(out, gout_ref[...])


def _proj_res(y, w, g_out, x):
    b, s, d = x.shape
    rows = ROW_TILE
    kdim = y.shape[-1]
    nbytes = w.size * 2 + 2 * rows * kdim * 2 + 4 * rows * d * 4
    return pl.pallas_call(
        _proj_res_kernel,
        out_shape=jax.ShapeDtypeStruct(x.shape, F32),
        grid=(b, s // rows),
        in_specs=[_row_tiled(rows, kdim), _resident(w.shape), _resident((1, d)), _row_tiled(rows, d)],
        out_specs=_row_tiled(rows, d),
        compiler_params=pltpu.CompilerParams(
            dimension_semantics=("arbitrary", "arbitrary"),
            vmem_limit_bytes=_vmem_limit(nbytes)),
        name="attn_out_proj",
    )(y, w, g_out.reshape(1, d), x)


def _attn_layer(x, g_in, g_out, w_qkv, w_o):
    n_pairs = w_qkv.shape[1] // (len(DILATED_CFG) * 3 * LANES)
    assert n_pairs * 2 == HEADS_PER_GROUP and 2 * HEAD_DIM == LANES
    qkv_groups = _qkv_proj(x, g_in, w_qkv.astype(BF16), n_pairs)
    merged = _attn_core(qkv_groups, _attn_bias_table(n_pairs))
    return _proj_res(merged, w_o.astype(BF16), g_out, x)


def kernel(x, norm_g, attn_w_qkv, attn_w_o, conv_w_in, conv_w_dw, conv_w_out,
           pool_w_in, pool_w_grp, pool_scale, pool_w_out, ffn_w_up, ffn_w_dw, ffn_w_down):
    depth = norm_g.shape[0]
    ia = ib = ic = 0
    for i in range(depth):
        g = norm_g[i]
        kind = i % 3
        if kind == 0:
            x = _attn_layer(x, g[0], g[1], attn_w_qkv[ia], attn_w_o[ia])
            ia += 1
        elif kind == 1:
            x = _conv_mixer_layer(x, g[0], g[1], conv_w_in[ib], conv_w_dw[ib], conv_w_out[ib])
            ib += 1
        else:
            x = _pool_mixer_layer(x, g[0], g[1], pool_w_in[ic], pool_w_grp[ic], pool_scale[ic],
                                  pool_w_out[ic])
            ic += 1
        x = _ffn_layer(x, g[2], g[3], ffn_w_up[i], ffn_w_dw[i], ffn_w_down[i])
    return x
```

```python
import functools

import numpy as np
import jax
import jax.numpy as jnp
from jax import lax
from jax.experimental import pallas as pl
from jax.experimental.pallas import tpu as pltpu

F32 = jnp.float32
BF16 = jnp.bfloat16

RMS_EPS = 1e-6
NEG_INF = -1e30

DILATED_CFG = ((128, 1), (512, 4), (2048, 16))
HEADS_PER_GROUP = 8
HEAD_DIM = 64
LOG2_E = 1.4426950408889634
Q_SCALE = HEAD_DIM ** -0.5 * LOG2_E
ATTN_BLOCK = 128
ATTN_UNROLL = 16
POOL_WINDOWS = (2, 4, 8, 16)

LANES = 128
MXU_DIM = 256
VMEM_CAP_BYTES = 60000 * 1024

ROW_TILE = 512
FFN_ROW_TILE = 1024
CONV_HALO = 8
POOL_HALO = 16


def _vmem_limit(nbytes):
    return int(min(VMEM_CAP_BYTES, nbytes + (16 << 20)))


def _rms(x, g):
    ms = jnp.mean(x * x, axis=-1, keepdims=True)
    return x * lax.rsqrt(ms + RMS_EPS) * g


def _aligned(v, m):
    return v if isinstance(v, int) else pl.multiple_of(v, m)


def _resident(shape):
    nd = len(shape)
    return pl.BlockSpec(shape, lambda *_: (0,) * nd, pipeline_mode=pl.Buffered(1))


def _row_tiled(rows, width):
    return pl.BlockSpec((1, rows, width), lambda bi, i: (bi, i, 0))


def _causal_conv3(cur, halo_ref, carry_ref, w, rows):
    out = []
    for s in range(cur.shape[1] // LANES):
        cols = slice(s * LANES, (s + 1) * LANES)
        piece = cur[:, cols]
        halo_ref[s, 0:CONV_HALO, :] = carry_ref[:, cols]
        halo_ref[s, CONV_HALO:CONV_HALO + rows, :] = piece
        carry_ref[:, cols] = piece[rows - CONV_HALO:rows, :]
        out.append(piece * w[2:3, cols]
                   + halo_ref[s, CONV_HALO - 1:CONV_HALO - 1 + rows, :] * w[1:2, cols]
                   + halo_ref[s, CONV_HALO - 2:CONV_HALO - 2 + rows, :] * w[0:1, cols])
    return jnp.concatenate(out, axis=1)


def _zero_at_sequence_start(carry_ref):
    @pl.when(pl.program_id(1) == 0)
    def _():
        carry_ref[...] = jnp.zeros(carry_ref.shape, carry_ref.dtype)


def _ffn_kernel(x_ref, gin_ref, wup_ref, wdw_ref, wdown_ref, gout_ref, o_ref,
                halo_ref, carry_ref, act_ref, *, rows, sub_rows, d_ff, chunk):
    _zero_at_sequence_start(carry_ref)
    for t in range(rows // sub_rows):
        tile = slice(t * sub_rows, (t + 1) * sub_rows)
        x = x_ref[0, tile, :]
        xn = _rms(x, gin_ref[...]).astype(BF16)
        for c in range(d_ff // chunk):
            halves = []
            for part in range(2):
                cols = slice(part * d_ff + c * chunk, part * d_ff + (c + 1) * chunk)
                h = jnp.dot(xn, wup_ref[:, cols], preferred_element_type=F32)
                slot = 2 * c + part
                halves.append(_causal_conv3(h, halo_ref.at[slot % 4], carry_ref.at[slot],
                                            wdw_ref[:, cols], sub_rows))
            gate, lin = halves
            act_ref[tile, c * chunk:(c + 1) * chunk] = (gate * jax.nn.sigmoid(gate) * lin).astype(BF16)
        out = jnp.dot(act_ref[tile, :], wdown_ref[...], preferred_element_type=F32)
        o_ref[0, tile, :] = x + _rms(out, gout_ref[...])


def _ffn_layer(x, g_in, g_out, w_up, w_dw, w_down):
    b, s, d = x.shape
    d_ff = w_down.shape[0]
    chunk = MXU_DIM
    rows, sub_rows = FFN_ROW_TILE, ROW_TILE
    assert d_ff % chunk == 0 and s % rows == 0 and rows % sub_rows == 0
    n_slots = 2 * (d_ff // chunk)
    nbytes = (w_up.size * 2 + w_down.size * 2 + 8 * w_dw.shape[1] * 4
              + 4 * rows * d * 4
              + 4 * (CONV_HALO + sub_rows) * chunk * 4 + n_slots * CONV_HALO * chunk * 4
              + rows * d_ff * 2)
    return pl.pallas_call(
        functools.partial(_ffn_kernel, rows=rows, sub_rows=sub_rows, d_ff=d_ff, chunk=chunk),
        out_shape=jax.ShapeDtypeStruct(x.shape, F32),
        grid=(b, s // rows),
        in_specs=[
            _row_tiled(rows, d),
            _resident((1, d)),
            _resident(w_up.shape),
            _resident(w_dw.shape),
            _resident(w_down.shape),
            _resident((1, d)),
        ],
        out_specs=_row_tiled(rows, d),
        scratch_shapes=[
            pltpu.VMEM((4, chunk // LANES, CONV_HALO + sub_rows, LANES), F32),
            pltpu.VMEM((n_slots, CONV_HALO, chunk), F32),
            pltpu.VMEM((rows, d_ff), BF16),
        ],
        compiler_params=pltpu.CompilerParams(
            dimension_semantics=("arbitrary", "arbitrary"),
            vmem_limit_bytes=_vmem_limit(nbytes)),
        name="conv_ffn",
    )(x, g_in.reshape(1, d), w_up.astype(BF16), w_dw, w_down.astype(BF16), g_out.reshape(1, d))


def _conv_mixer_kernel(x_ref, gin_ref, win_ref, wdw_ref, wout_ref, gout_ref, o_ref,
                       halo_ref, carry_ref, y_ref, *, rows, d, chunk):
    _zero_at_sequence_start(carry_ref)
    x = x_ref[0]
    xn = _rms(x, gin_ref[...]).astype(BF16)
    for c in range(d // chunk):
        cols = slice(c * chunk, (c + 1) * chunk)
        gate_b, gate_c, h = [
            jnp.dot(xn, win_ref[:, part * d + c * chunk:part * d + (c + 1) * chunk],
                    preferred_element_type=F32) for part in range(3)]
        conv = _causal_conv3(gate_c * h, halo_ref.at[c % 2], carry_ref.at[c], wdw_ref[:, cols], rows)
        y_ref[:, cols] = (gate_b * conv).astype(BF16)
    out = jnp.dot(y_ref[...], wout_ref[...], preferred_element_type=F32)
    o_ref[0] = x + _rms(out, gout_ref[...])


def _conv_mixer_layer(x, g_in, g_out, w_in, w_dw, w_out):
    b, s, d = x.shape
    chunk = MXU_DIM
    rows = ROW_TILE
    n_chunks = d // chunk
    nbytes = (w_in.size * 2 + w_out.size * 2 + 8 * d * 4 + 4 * rows * d * 4
              + 2 * (CONV_HALO + rows) * chunk * 4 + n_chunks * CONV_HALO * chunk * 4
              + rows * d * 2)
    return pl.pallas_call(
        functools.partial(_conv_mixer_kernel, rows=rows, d=d, chunk=chunk),
        out_shape=jax.ShapeDtypeStruct(x.shape, F32),
        grid=(b, s // rows),
        in_specs=[
            _row_tiled(rows, d),
            _resident((1, d)),
            _resident(w_in.shape),
            _resident(w_dw.shape),
            _resident(w_out.shape),
            _resident((1, d)),
        ],
        out_specs=_row_tiled(rows, d),
        scratch_shapes=[
            pltpu.VMEM((2, chunk // LANES, CONV_HALO + rows, LANES), F32),
            pltpu.VMEM((n_chunks, CONV_HALO, chunk), F32),
            pltpu.VMEM((rows, d), BF16),
        ],
        compiler_params=pltpu.CompilerParams(
            dimension_semantics=("arbitrary", "arbitrary"),
            vmem_limit_bytes=_vmem_limit(nbytes)),
        name="short_conv_mixer",
    )(x, g_in.reshape(1, d), w_in.astype(BF16), w_dw, w_out.astype(BF16), g_out.reshape(1, d))


def _pool_mixer_kernel(x_ref, gin_ref, win_ref, wgrp_ref, scale_ref, wout_ref, gout_ref, o_ref,
                       halo_ref, carry_ref, y_ref, *, rows, group_dim):
    _zero_at_sequence_start(carry_ref)
    x = x_ref[0]
    xn = _rms(x, gin_ref[...]).astype(BF16)
    u = jnp.dot(xn, win_ref[...], preferred_element_type=F32)
    pos = pl.program_id(1) * rows + 1 + lax.broadcasted_iota(jnp.int32, (rows, LANES), 0)
    slabs_per_group = group_dim // LANES
    for g, w in enumerate(POOL_WINDOWS):
        inv_count = 1.0 / jnp.minimum(pos, w).astype(F32)
        pooled = []
        for s in range(g * slabs_per_group, (g + 1) * slabs_per_group):
            cols = slice(s * LANES, (s + 1) * LANES)
            us = u[:, cols]
            halo_ref[s, 0:POOL_HALO, :] = carry_ref[:, cols]
            halo_ref[s, POOL_HALO:POOL_HALO + rows, :] = us
            carry_ref[:, cols] = us[rows - POOL_HALO:rows, :]
            total = us
            for k in range(1, w):
                total = total + halo_ref[s, POOL_HALO - k:POOL_HALO - k + rows, :]
            pooled.append((total * inv_count - us).astype(BF16))
        cols = slice(g * group_dim, (g + 1) * group_dim)
        yg = jnp.dot(jnp.concatenate(pooled, axis=1), wgrp_ref[g], preferred_element_type=F32)
        y_ref[:, cols] = (yg * scale_ref[:, cols]).astype(BF16)
    out = jnp.dot(y_ref[...], wout_ref[...], preferred_element_type=F32)
    o_ref[0] = x + _rms(out, gout_ref[...])


def _pool_mixer_layer(x, g_in, g_out, w_in, w_grp, scale, w_out):
    b, s, d = x.shape
    n_groups, group_dim, _ = w_grp.shape
    assert n_groups == len(POOL_WINDOWS) and n_groups * group_dim == d
    rows = ROW_TILE
    nbytes = (w_in.size * 2 + w_grp.size * 2 + w_out.size * 2 + 4 * rows * d * 4
              + (POOL_HALO + rows) * d * 4 + POOL_HALO * d * 4 + rows * d * 2)
    return pl.pallas_call(
        functools.partial(_pool_mixer_kernel, rows=rows, group_dim=group_dim),
        out_shape=jax.ShapeDtypeStruct(x.shape, F32),
        grid=(b, s // rows),
        in_specs=[
            _row_tiled(rows, d),
            _resident((1, d)),
            _resident(w_in.shape),
            _resident(w_grp.shape),
            _resident((1, d)),
            _resident(w_out.shape),
            _resident((1, d)),
        ],
        out_specs=_row_tiled(rows, d),
        scratch_shapes=[
            pltpu.VMEM((d // LANES, POOL_HALO + rows, LANES), F32),
            pltpu.VMEM((POOL_HALO, d), F32),
            pltpu.VMEM((rows, d), BF16),
        ],
        compiler_params=pltpu.CompilerParams(
            dimension_semantics=("arbitrary", "arbitrary"),
            vmem_limit_bytes=_vmem_limit(nbytes)),
        name="pool_mixer",
    )(x, g_in.reshape(1, d), w_in.astype(BF16), w_grp.astype(BF16), scale.reshape(1, d),
      w_out.astype(BF16), g_out.reshape(1, d))


def _qkv_kernel(x_ref, gin_ref, w_ref, *refs, rows, d, n_pairs):
    n_groups = len(DILATED_CFG)
    outs, (slab_ref, perm_ref) = refs[:n_groups], refs[n_groups:]
    width = 3 * n_pairs * LANES
    xn = _rms(x_ref[0], gin_ref[...])
    n_slabs = d // LANES
    for s in range(n_slabs):
        slab_ref[s] = xn[:, s * LANES:(s + 1) * LANES]
    for g, (_, dilation) in enumerate(DILATED_CFG):
        sub = rows // dilation
        if dilation == 1:
            xg = xn.astype(BF16)
        else:
            perm = perm_ref.at[g % 2]
            for r in range(dilation):
                for s in range(n_slabs):
                    perm[r * sub:(r + 1) * sub, s * LANES:(s + 1) * LANES] = (
                        slab_ref[s, pl.ds(r, sub, stride=dilation), :].astype(BF16))
            xg = perm[...]
        y = jnp.dot(xg, w_ref[:, g * width:(g + 1) * width], preferred_element_type=F32)
        for i in range(3):
            for hp in range(n_pairs):
                lo = (i * n_pairs + hp) * LANES
                part = y[:, lo:lo + LANES]
                if i == 0:
                    part = part * Q_SCALE
                for r in range(dilation):
                    outs[g][0, i, hp, r] = part[r * sub:(r + 1) * sub, :].astype(BF16)


def _qkv_proj(x, g_in, w_qkv, n_pairs):
    b, s, d = x.shape
    rows = ROW_TILE
    out_shapes, out_specs = [], []
    for _, dilation in DILATED_CFG:
        length = s // dilation
        out_shapes.append(jax.ShapeDtypeStruct((b, 3, n_pairs, dilation, length, LANES), BF16))
        out_specs.append(pl.BlockSpec((1, 3, n_pairs, dilation, rows // dilation, LANES),
                                      lambda bi, j: (bi, 0, 0, 0, j, 0)))
    nbytes = (w_qkv.size * 2 + 2 * rows * d * 4 + 2 * rows * w_qkv.shape[1] * 2
              + rows * d * 4 + 2 * rows * d * 2)
    outs = pl.pallas_call(
        functools.partial(_qkv_kernel, rows=rows, d=d, n_pairs=n_pairs),
        out_shape=out_shapes,
        grid=(b, s // rows),
        in_specs=[_row_tiled(rows, d), _resident((1, d)), _resident(w_qkv.shape)],
        out_specs=out_specs,
        scratch_shapes=[
            pltpu.VMEM((d // LANES, rows, LANES), F32),
            pltpu.VMEM((2, rows, d), BF16),
        ],
        compiler_params=pltpu.CompilerParams(
            dimension_semantics=("arbitrary", "arbitrary"),
            vmem_limit_bytes=_vmem_limit(nbytes)),
        name="qkv_proj",
    )(x, g_in.reshape(1, d), w_qkv)
    return [o.reshape(b, 3, n_pairs, s, LANES) for o in outs]


def _attn_bias_table(n_pairs):
    n_heads = len(DILATED_CFG) * HEADS_PER_GROUP
    slopes = (2.0 ** (-8.0 * np.arange(1, n_heads + 1) / n_heads)).astype(np.float32)
    delta = ATTN_BLOCK + np.arange(ATTN_BLOCK)[:, None] - np.arange(2 * ATTN_BLOCK)[None, :]
    neg = np.float32(NEG_INF)
    tables = []
    for g, (window, dilation) in enumerate(DILATED_CFG):
        span = window // dilation
        valid = (delta >= 0) & (delta <= span)
        dist = (delta * dilation).astype(np.float32)
        with_prev = np.stack([np.where(valid, -slopes[g * HEADS_PER_GROUP + h] * dist * np.float32(LOG2_E), neg)
                              for h in range(HEADS_PER_GROUP)])
        with_prev = with_prev.reshape(n_pairs, 2, ATTN_BLOCK, 2 * ATTN_BLOCK)
        no_prev = with_prev.copy()
        no_prev[..., :ATTN_BLOCK] = neg
        row_zero = np.full_like(with_prev, neg)
        row_zero[..., :ATTN_BLOCK] = with_prev[..., ATTN_BLOCK:]
        tables += [with_prev, no_prev, row_zero]
    return jnp.asarray(np.stack(tables), F32)


WITH_PREV, NO_PREV, ROW_ZERO = 0, 1, 2
N_BIAS_KINDS = 3


def _attn_kernel(q0, k0, v0, q1, k1, v1, q2, k2, v2, bias_ref, o_ref, acc_ref, m_ref, l_ref, *, seq):
    blk = ATTN_BLOCK
    low_half = lax.broadcasted_iota(jnp.int32, (blk, LANES), 1) < HEAD_DIM

    def scores(kind, q_ref, k_ref, v_ref, q_start, k_start):
        q = q_ref[pl.ds(_aligned(q_start, blk), blk), :]
        k = k_ref[pl.ds(_aligned(k_start, blk), 2 * blk), :]
        v = v_ref[pl.ds(_aligned(k_start, blk), 2 * blk), :]
        v_ext = jnp.concatenate([v, jnp.ones((2 * blk, LANES), BF16)], axis=1)
        m_h, l_h, pv_h = [], [], []
        for a in range(2):
            qa = jnp.where(low_half if a == 0 else ~low_half, q, jnp.zeros_like(q))
            s = lax.dot_general(qa, k, (((1,), (1,)), ((), ())), preferred_element_type=F32)
            s = s + bias_ref[kind, a]
            m = jnp.max(s, axis=-1, keepdims=True)
            p = jnp.exp2(s - m).astype(BF16)
            pvl = jnp.dot(p, v_ext, preferred_element_type=F32)
            m_h.append(m)
            pv_h.append(pvl[:, :LANES])
            l_h.append(pvl[:, LANES:])
        return (jnp.where(low_half, m_h[0], m_h[1]), jnp.where(low_half, l_h[0], l_h[1]),
                jnp.where(low_half, pv_h[0], pv_h[1]))

    def merge(out_rows, first_group, m_b, l_b, pv_b):
        if first_group:
            m_ref[out_rows, :] = m_b
            l_ref[out_rows, :] = l_b
            acc_ref[out_rows, :] = pv_b
        else:
            m_old = m_ref[out_rows, :]
            m_new = jnp.maximum(m_old, m_b)
            a_old = jnp.exp2(m_old - m_new)
            a_blk = jnp.exp2(m_b - m_new)
            m_ref[out_rows, :] = m_new
            l_ref[out_rows, :] = l_ref[out_rows, :] * a_old + l_b * a_blk
            acc_ref[out_rows, :] = acc_ref[out_rows, :] * a_old + pv_b * a_blk

    groups = ((q0, k0, v0), (q1, k1, v1), (q2, k2, v2))
    total_blocks = seq // blk
    order = sorted(range(len(DILATED_CFG)), key=lambda g: -DILATED_CFG[g][1])
    for g in order:
        dilation, refs, first_group = DILATED_CFG[g][1], groups[g], g == order[0]
        n_blocks = total_blocks // dilation
        assert n_blocks & (n_blocks - 1) == 0 and total_blocks % ATTN_UNROLL == 0

        def step(it, carry, g=g, dilation=dilation, refs=refs, first_group=first_group, n_blocks=n_blocks):
            work = []
            for j in range(ATTN_UNROLL):
                f = it * ATTN_UNROLL + j
                r = lax.shift_right_logical(f, n_blocks.bit_length() - 1)
                n = lax.bitwise_and(f, n_blocks - 1)
                kind = jnp.where(n == 0, jnp.where(f == 0, ROW_ZERO, NO_PREV), WITH_PREV)
                start = n * (blk * dilation) + r
                rows = (pl.ds(_aligned(start, blk), blk) if dilation == 1
                        else pl.ds(start, blk, stride=dilation))
                work.append((scores(g * N_BIAS_KINDS + kind, *refs, f * blk, jnp.maximum(f - 1, 0) * blk),
                             rows))
            for vals, rows in work:
                merge(rows, first_group, *vals)
            return carry

        lax.fori_loop(0, total_blocks // ATTN_UNROLL, step, 0)

    o_ref[0] = (acc_ref[...] / l_ref[...]).astype(BF16)


def _attn_core(qkv_groups, bias):
    b, _, n_pairs, s, _ = qkv_groups[0].shape
    in_specs, args = [], []
    for qkv in qkv_groups:
        for i in range(3):
            in_specs.append(pl.BlockSpec((None, None, None, s, LANES),
                                         lambda bi, hp, i=i: (bi, i, hp, 0, 0)))
            args.append(qkv)
    n_kinds = bias.shape[0]
    in_specs.append(pl.BlockSpec((n_kinds, None, 2, ATTN_BLOCK, 2 * ATTN_BLOCK),
                                 lambda bi, hp: (0, hp, 0, 0, 0)))
    nbytes = (2 * 9 * s * LANES * 2 + 2 * n_kinds * 2 * ATTN_BLOCK * 2 * ATTN_BLOCK * 4
              + 2 * s * LANES * 2 + 3 * s * LANES * 4)
    return pl.pallas_call(
        functools.partial(_attn_kernel, seq=s),
        out_shape=jax.ShapeDtypeStruct((b, s, n_pairs * LANES), BF16),
        grid=(b, n_pairs),
        in_specs=in_specs,
        out_specs=pl.BlockSpec((1, s, LANES), lambda bi, hp: (bi, 0, hp)),
        scratch_shapes=[pltpu.VMEM((s, LANES), F32)] * 3,
        compiler_params=pltpu.CompilerParams(
            dimension_semantics=("arbitrary", "arbitrary"),
            vmem_limit_bytes=_vmem_limit(nbytes)),
        name="dilated_attn",
    )(*args, bias)


def _proj_res_kernel(y_ref, w_ref, gout_ref, x_ref, o_ref):
    out = jnp.dot(y_ref[0], w_ref[...], preferred_element_type=F32)
    o_ref[0] = x_ref[0] + _rms(out, gout_ref[...])


def _proj_res(y, w, g_out, x):
    b, s, d = x.shape
    rows = ROW_TILE
    kdim = y.shape[-1]
    nbytes = w.size * 2 + 2 * rows * kdim * 2 + 4 * rows * d * 4
    return pl.pallas_call(
        _proj_res_kernel,
        out_shape=jax.ShapeDtypeStruct(x.shape, F32),
        grid=(b, s // rows),
        in_specs=[_row_tiled(rows, kdim), _resident(w.shape), _resident((1, d)), _row_tiled(rows, d)],
        out_specs=_row_tiled(rows, d),
        compiler_params=pltpu.CompilerParams(
            dimension_semantics=("arbitrary", "arbitrary"),
            vmem_limit_bytes=_vmem_limit(nbytes)),
        name="attn_out_proj",
    )(y, w, g_out.reshape(1, d), x)


def _attn_layer(x, g_in, g_out, w_qkv, w_o):
    n_pairs = w_qkv.shape[1] // (len(DILATED_CFG) * 3 * LANES)
    assert n_pairs * 2 == HEADS_PER_GROUP and 2 * HEAD_DIM == LANES
    qkv_groups = _qkv_proj(x, g_in, w_qkv.astype(BF16), n_pairs)
    merged = _attn_core(qkv_groups, _attn_bias_table(n_pairs))
    return _proj_res(merged, w_o.astype(BF16), g_out, x)


def kernel(x, norm_g, attn_w_qkv, attn_w_o, conv_w_in, conv_w_dw, conv_w_out,
           pool_w_in, pool_w_grp, pool_scale, pool_w_out, ffn_w_up, ffn_w_dw, ffn_w_down):
    depth = norm_g.shape[0]
    ia = ib = ic = 0
    for i in range(depth):
        g = norm_g[i]
        kind = i % 3
        if kind == 0:
            x = _attn_layer(x, g[0], g[1], attn_w_qkv[ia], attn_w_o[ia])
            ia += 1
        elif kind == 1:
            x = _conv_mixer_layer(x, g[0], g[1], conv_w_in[ib], conv_w_dw[ib], conv_w_out[ib])
            ib += 1
        else:
            x = _pool_mixer_layer(x, g[0], g[1], pool_w_in[ic], pool_w_grp[ic], pool_scale[ic],
                                  pool_w_out[ic])
            ic += 1
        x = _ffn_layer(x, g[2], g[3], ffn_w_up[i], ffn_w_dw[i], ffn_w_down[i])
    return x
```

```python
import functools

import numpy as np
import jax
import jax.numpy as jnp
from jax import lax
from jax.experimental import pallas as pl
from jax.experimental.pallas import tpu as pltpu

F32 = jnp.float32
BF16 = jnp.bfloat16

RMS_EPS = 1e-6
NEG_INF = -1e30

DILATED_CFG = ((128, 1), (512, 4), (2048, 16))
HEADS_PER_GROUP = 8
HEAD_DIM = 64
LOG2_E = 1.4426950408889634
Q_SCALE = HEAD_DIM ** -0.5 * LOG2_E
ATTN_BLOCK = 128
ATTN_UNROLL = 16
POOL_WINDOWS = (2, 4, 8, 16)

LANES = 128
MXU_DIM = 256
VMEM_CAP_BYTES = 60000 * 1024

ROW_TILE = 512
BIG_ROW_TILE = 1024
CONV_HALO = 8
POOL_HALO = 16


def _vmem_limit(nbytes):
    return int(min(VMEM_CAP_BYTES, nbytes + (16 << 20)))


def _rms(x, g):
    ms = jnp.mean(x * x, axis=-1, keepdims=True)
    return x * lax.rsqrt(ms + RMS_EPS) * g


def _aligned(v, m):
    return v if isinstance(v, int) else pl.multiple_of(v, m)


def _resident(shape):
    nd = len(shape)
    return pl.BlockSpec(shape, lambda *_: (0,) * nd, pipeline_mode=pl.Buffered(1))


def _row_tiled(rows, width):
    return pl.BlockSpec((1, rows, width), lambda bi, i: (bi, i, 0))


def _causal_conv3(cur, halo_ref, carry_ref, w, rows):
    out = []
    for s in range(cur.shape[1] // LANES):
        cols = slice(s * LANES, (s + 1) * LANES)
        piece = cur[:, cols]
        halo_ref[s, 0:CONV_HALO, :] = carry_ref[:, cols]
        halo_ref[s, CONV_HALO:CONV_HALO + rows, :] = piece
        carry_ref[:, cols] = piece[rows - CONV_HALO:rows, :]
        out.append(piece * w[2:3, cols]
                   + halo_ref[s, CONV_HALO - 1:CONV_HALO - 1 + rows, :] * w[1:2, cols]
                   + halo_ref[s, CONV_HALO - 2:CONV_HALO - 2 + rows, :] * w[0:1, cols])
    return jnp.concatenate(out, axis=1)


def _zero_at_sequence_start(carry_ref):
    @pl.when(pl.program_id(1) == 0)
    def _():
        carry_ref[...] = jnp.zeros(carry_ref.shape, carry_ref.dtype)


def _layer_tail_kernel(x_ref, y_ref, wmix_ref, gmix_ref, gin_ref, wup_ref, wdw_ref, wdown_ref, gout_ref,
                       o_ref, halo_ref, carry_ref, act_ref, *, rows, sub_rows, d_ff, chunk):
    _zero_at_sequence_start(carry_ref)
    for t in range(rows // sub_rows):
        tile = slice(t * sub_rows, (t + 1) * sub_rows)
        mixed = jnp.dot(y_ref[0, tile, :], wmix_ref[...], preferred_element_type=F32)
        x = x_ref[0, tile, :] + _rms(mixed, gmix_ref[...])
        xn = _rms(x, gin_ref[...]).astype(BF16)
        for c in range(d_ff // chunk):
            halves = []
            for part in range(2):
                cols = slice(part * d_ff + c * chunk, part * d_ff + (c + 1) * chunk)
                h = jnp.dot(xn, wup_ref[:, cols], preferred_element_type=F32)
                slot = 2 * c + part
                halves.append(_causal_conv3(h, halo_ref.at[slot % 4], carry_ref.at[slot],
                                            wdw_ref[:, cols], sub_rows))
            gate, lin = halves
            act_ref[tile, c * chunk:(c + 1) * chunk] = (gate * jax.nn.sigmoid(gate) * lin).astype(BF16)
        out = jnp.dot(act_ref[tile, :], wdown_ref[...], preferred_element_type=F32)
        o_ref[0, tile, :] = x + _rms(out, gout_ref[...])


def _layer_tail(x, y, w_mix, g_mix, g_in, g_out, w_up, w_dw, w_down):
    b, s, d = x.shape
    d_ff = w_down.shape[0]
    d_mix = y.shape[-1]
    chunk = MXU_DIM
    rows, sub_rows = BIG_ROW_TILE, ROW_TILE
    assert d_ff % chunk == 0 and s % rows == 0 and rows % sub_rows == 0
    n_slots = 2 * (d_ff // chunk)
    nbytes = (w_up.size * 2 + w_down.size * 2 + w_mix.size * 2 + 8 * w_dw.shape[1] * 4
              + 4 * rows * d * 4 + 2 * rows * d_mix * 2
              + 4 * (CONV_HALO + sub_rows) * chunk * 4 + n_slots * CONV_HALO * chunk * 4
              + rows * d_ff * 2)
    return pl.pallas_call(
        functools.partial(_layer_tail_kernel, rows=rows, sub_rows=sub_rows, d_ff=d_ff, chunk=chunk),
        out_shape=jax.ShapeDtypeStruct(x.shape, F32),
        grid=(b, s // rows),
        in_specs=[
            _row_tiled(rows, d),
            _row_tiled(rows, d_mix),
            _resident(w_mix.shape),
            _resident((1, d)),
            _resident((1, d)),
            _resident(w_up.shape),
            _resident(w_dw.shape),
            _resident(w_down.shape),
            _resident((1, d)),
        ],
        out_specs=_row_tiled(rows, d),
        scratch_shapes=[
            pltpu.VMEM((4, chunk // LANES, CONV_HALO + sub_rows, LANES), F32),
            pltpu.VMEM((n_slots, CONV_HALO, chunk), F32),
            pltpu.VMEM((rows, d_ff), BF16),
        ],
        compiler_params=pltpu.CompilerParams(
            dimension_semantics=("arbitrary", "arbitrary"),
            vmem_limit_bytes=_vmem_limit(nbytes)),
        name="layer_tail",
    )(x, y, w_mix.astype(BF16), g_mix.reshape(1, d), g_in.reshape(1, d), w_up.astype(BF16), w_dw,
      w_down.astype(BF16), g_out.reshape(1, d))


def _conv_mixer_kernel(x_ref, gin_ref, win_ref, wdw_ref, o_ref, halo_ref, carry_ref,
                       *, rows, sub_rows, d, chunk):
    _zero_at_sequence_start(carry_ref)
    for t in range(rows // sub_rows):
        tile = slice(t * sub_rows, (t + 1) * sub_rows)
        xn = _rms(x_ref[0, tile, :], gin_ref[...]).astype(BF16)
        for c in range(d // chunk):
            cols = slice(c * chunk, (c + 1) * chunk)
            gate_b, gate_c, h = [
                jnp.dot(xn, win_ref[:, part * d + c * chunk:part * d + (c + 1) * chunk],
                        preferred_element_type=F32) for part in range(3)]
            conv = _causal_conv3(gate_c * h, halo_ref.at[c % 2], carry_ref.at[c], wdw_ref[:, cols],
                                 sub_rows)
            o_ref[0, tile, cols] = (gate_b * conv).astype(BF16)


def _conv_mixer(x, g_in, w_in, w_dw):
    b, s, d = x.shape
    chunk = MXU_DIM
    rows, sub_rows = BIG_ROW_TILE, ROW_TILE
    n_chunks = d // chunk
    nbytes = (w_in.size * 2 + 8 * d * 4 + 2 * rows * d * 4 + 2 * rows * d * 2
              + 2 * (CONV_HALO + sub_rows) * chunk * 4 + n_chunks * CONV_HALO * chunk * 4)
    return pl.pallas_call(
        functools.partial(_conv_mixer_kernel, rows=rows, sub_rows=sub_rows, d=d, chunk=chunk),
        out_shape=jax.ShapeDtypeStruct(x.shape, BF16),
        grid=(b, s // rows),
        in_specs=[
            _row_tiled(rows, d),
            _resident((1, d)),
            _resident(w_in.shape),
            _resident(w_dw.shape),
        ],
        out_specs=_row_tiled(rows, d),
        scratch_shapes=[
            pltpu.VMEM((2, chunk // LANES, CONV_HALO + sub_rows, LANES), F32),
            pltpu.VMEM((n_chunks, CONV_HALO, chunk), F32),
        ],
        compiler_params=pltpu.CompilerParams(
            dimension_semantics=("arbitrary", "arbitrary"),
            vmem_limit_bytes=_vmem_limit(nbytes)),
        name="short_conv_mixer",
    )(x, g_in.reshape(1, d), w_in.astype(BF16), w_dw)


def _pool_mixer_kernel(x_ref, gin_ref, win_ref, wgrp_ref, scale_ref, o_ref, halo_ref, carry_ref,
                       *, rows, sub_rows, group_dim):
    _zero_at_sequence_start(carry_ref)
    slabs_per_group = group_dim // LANES
    for t in range(rows // sub_rows):
        tile = slice(t * sub_rows, (t + 1) * sub_rows)
        xn = _rms(x_ref[0, tile, :], gin_ref[...]).astype(BF16)
        u = jnp.dot(xn, win_ref[...], preferred_element_type=F32)
        pos = (pl.program_id(1) * rows + t * sub_rows + 1
               + lax.broadcasted_iota(jnp.int32, (sub_rows, LANES), 0))
        for g, w in enumerate(POOL_WINDOWS):
            inv_count = 1.0 / jnp.minimum(pos, w).astype(F32)
            pooled = []
            for s in range(g * slabs_per_group, (g + 1) * slabs_per_group):
                cols = slice(s * LANES, (s + 1) * LANES)
                us = u[:, cols]
                halo = halo_ref.at[t % 2, s]
                halo[0:POOL_HALO, :] = carry_ref[:, cols]
                halo[POOL_HALO:POOL_HALO + sub_rows, :] = us
                carry_ref[:, cols] = us[sub_rows - POOL_HALO:sub_rows, :]
                total = us
                for k in range(1, w):
                    total = total + halo[POOL_HALO - k:POOL_HALO - k + sub_rows, :]
                pooled.append((total * inv_count - us).astype(BF16))
            cols = slice(g * group_dim, (g + 1) * group_dim)
            yg = jnp.dot(jnp.concatenate(pooled, axis=1), wgrp_ref[g], preferred_element_type=F32)
            o_ref[0, tile, cols] = (yg * scale_ref[:, cols]).astype(BF16)


def _pool_mixer(x, g_in, w_in, w_grp, scale):
    b, s, d = x.shape
    n_groups, group_dim, _ = w_grp.shape
    assert n_groups == len(POOL_WINDOWS) and n_groups * group_dim == d
    rows, sub_rows = BIG_ROW_TILE, ROW_TILE
    nbytes = (w_in.size * 2 + w_grp.size * 2 + 2 * rows * d * 4 + 2 * rows * d * 2
              + 2 * (POOL_HALO + sub_rows) * d * 4 + POOL_HALO * d * 4)
    return pl.pallas_call(
        functools.partial(_pool_mixer_kernel, rows=rows, sub_rows=sub_rows, group_dim=group_dim),
        out_shape=jax.ShapeDtypeStruct(x.shape, BF16),
        grid=(b, s // rows),
        in_specs=[
            _row_tiled(rows, d),
            _resident((1, d)),
            _resident(w_in.shape),
            _resident(w_grp.shape),
            _resident((1, d)),
        ],
        out_specs=_row_tiled(rows, d),
        scratch_shapes=[
            pltpu.VMEM((2, d // LANES, POOL_HALO + sub_rows, LANES), F32),
            pltpu.VMEM((POOL_HALO, d), F32),
        ],
        compiler_params=pltpu.CompilerParams(
            dimension_semantics=("arbitrary", "arbitrary"),
            vmem_limit_bytes=_vmem_limit(nbytes)),
        name="pool_mixer",
    )(x, g_in.reshape(1, d), w_in.astype(BF16), w_grp.astype(BF16), scale.reshape(1, d))


def _qkv_kernel(x_ref, gin_ref, w_ref, *refs, rows, sub_rows, d, n_pairs):
    n_groups = len(DILATED_CFG)
    outs, (slab_ref, perm_ref) = refs[:n_groups], refs[n_groups:]
    width = 3 * n_pairs * LANES
    n_slabs = d // LANES
    for t in range(rows // sub_rows):
        xn = _rms(x_ref[0, t * sub_rows:(t + 1) * sub_rows, :], gin_ref[...])
        slab = slab_ref.at[t % 2]
        for s in range(n_slabs):
            slab[s] = xn[:, s * LANES:(s + 1) * LANES]
        for g, (_, dilation) in enumerate(DILATED_CFG):
            sub = sub_rows // dilation
            if dilation == 1:
                xg = xn.astype(BF16)
            else:
                perm = perm_ref.at[t % 2, g % 2]
                for r in range(dilation):
                    for s in range(n_slabs):
                        perm[r * sub:(r + 1) * sub, s * LANES:(s + 1) * LANES] = (
                            slab[s, pl.ds(r, sub, stride=dilation), :].astype(BF16))
                xg = perm[...]
            y = jnp.dot(xg, w_ref[:, g * width:(g + 1) * width], preferred_element_type=F32)
            for i in range(3):
                for hp in range(n_pairs):
                    lo = (i * n_pairs + hp) * LANES
                    part = y[:, lo:lo + LANES]
                    if i == 0:
                        part = part * Q_SCALE
                    for r in range(dilation):
                        outs[g][0, i, hp, r, t * sub:(t + 1) * sub, :] = (
                            part[r * sub:(r + 1) * sub, :].astype(BF16))


def _qkv_proj(x, g_in, w_qkv, n_pairs):
    b, s, d = x.shape
    rows, sub_rows = BIG_ROW_TILE, ROW_TILE
    out_shapes, out_specs = [], []
    for _, dilation in DILATED_CFG:
        length = s // dilation
        out_shapes.append(jax.ShapeDtypeStruct((b, 3, n_pairs, dilation, length, LANES), BF16))
        out_specs.append(pl.BlockSpec((1, 3, n_pairs, dilation, rows // dilation, LANES),
                                      lambda bi, j: (bi, 0, 0, 0, j, 0)))
    nbytes = (w_qkv.size * 2 + 2 * rows * d * 4 + 2 * rows * w_qkv.shape[1] * 2
              + 2 * sub_rows * d * 4 + 4 * sub_rows * d * 2)
    outs = pl.pallas_call(
        functools.partial(_qkv_kernel, rows=rows, sub_rows=sub_rows, d=d, n_pairs=n_pairs),
        out_shape=out_shapes,
        grid=(b, s // rows),
        in_specs=[_row_tiled(rows, d), _resident((1, d)), _resident(w_qkv.shape)],
        out_specs=out_specs,
        scratch_shapes=[
            pltpu.VMEM((2, d // LANES, sub_rows, LANES), F32),
            pltpu.VMEM((2, 2, sub_rows, d), BF16),
        ],
        compiler_params=pltpu.CompilerParams(
            dimension_semantics=("arbitrary", "arbitrary"),
            vmem_limit_bytes=_vmem_limit(nbytes)),
        name="qkv_proj",
    )(x, g_in.reshape(1, d), w_qkv)
    return [o.reshape(b, 3, n_pairs, s, LANES) for o in outs]


def _attn_bias_table(n_pairs):
    n_heads = len(DILATED_CFG) * HEADS_PER_GROUP
    slopes = (2.0 ** (-8.0 * np.arange(1, n_heads + 1) / n_heads)).astype(np.float32)
    delta = ATTN_BLOCK + np.arange(ATTN_BLOCK)[:, None] - np.arange(2 * ATTN_BLOCK)[None, :]
    neg = np.float32(NEG_INF)
    tables = []
    for g, (window, dilation) in enumerate(DILATED_CFG):
        span = window // dilation
        valid = (delta >= 0) & (delta <= span)
        dist = (delta * dilation).astype(np.float32)
        with_prev = np.stack([np.where(valid, -slopes[g * HEADS_PER_GROUP + h] * dist * np.float32(LOG2_E), neg)
                              for h in range(HEADS_PER_GROUP)])
        with_prev = with_prev.reshape(n_pairs, 2, ATTN_BLOCK, 2 * ATTN_BLOCK)
        no_prev = with_prev.copy()
        no_prev[..., :ATTN_BLOCK] = neg
        row_zero = np.full_like(with_prev, neg)
        row_zero[..., :ATTN_BLOCK] = with_prev[..., ATTN_BLOCK:]
        tables += [with_prev, no_prev, row_zero]
    return jnp.asarray(np.stack(tables), F32)


WITH_PREV, NO_PREV, ROW_ZERO = 0, 1, 2
N_BIAS_KINDS = 3


def _attn_kernel(q0, k0, v0, q1, k1, v1, q2, k2, v2, bias_ref, o_ref, acc_ref, m_ref, l_ref, *, seq):
    blk = ATTN_BLOCK
    low_half = lax.broadcasted_iota(jnp.int32, (blk, LANES), 1) < HEAD_DIM

    def scores(kind, q_ref, k_ref, v_ref, q_start, k_start):
        q = q_ref[pl.ds(_aligned(q_start, blk), blk), :]
        k = k_ref[pl.ds(_aligned(k_start, blk), 2 * blk), :]
        v = v_ref[pl.ds(_aligned(k_start, blk), 2 * blk), :]
        v_ext = jnp.concatenate([v, jnp.ones((2 * blk, LANES), BF16)], axis=1)
        m_h, l_h, pv_h = [], [], []
        for a in range(2):
            qa = jnp.where(low_half if a == 0 else ~low_half, q, jnp.zeros_like(q))
            s = lax.dot_general(qa, k, (((1,), (1,)), ((), ())), preferred_element_type=F32)
            s = s + bias_ref[kind, a]
            m = jnp.max(s, axis=-1, keepdims=True)
            p = jnp.exp2(s - m).astype(BF16)
            pvl = jnp.dot(p, v_ext, preferred_element_type=F32)
            m_h.append(m)
            pv_h.append(pvl[:, :LANES])
            l_h.append(pvl[:, LANES:])
        return (jnp.where(low_half, m_h[0], m_h[1]), jnp.where(low_half, l_h[0], l_h[1]),
                jnp.where(low_half, pv_h[0], pv_h[1]))

    def merge(out_rows, first_group, m_b, l_b, pv_b):
        if first_group:
            m_ref[out_rows, :] = m_b
            l_ref[out_rows, :] = l_b
            acc_ref[out_rows, :] = pv_b
        else:
            m_old = m_ref[out_rows, :]
            m_new = jnp.maximum(m_old, m_b)
            a_old = jnp.exp2(m_old - m_new)
            a_blk = jnp.exp2(m_b - m_new)
            m_ref[out_rows, :] = m_new
            l_ref[out_rows, :] = l_ref[out_rows, :] * a_old + l_b * a_blk
            acc_ref[out_rows, :] = acc_ref[out_rows, :] * a_old + pv_b * a_blk

    groups = ((q0, k0, v0), (q1, k1, v1), (q2, k2, v2))
    total_blocks = seq // blk
    order = sorted(range(len(DILATED_CFG)), key=lambda g: -DILATED_CFG[g][1])
    for g in order:
        dilation, refs, first_group = DILATED_CFG[g][1], groups[g], g == order[0]
        n_blocks = total_blocks // dilation
        assert n_blocks & (n_blocks - 1) == 0 and total_blocks % ATTN_UNROLL == 0

        def step(it, carry, g=g, dilation=dilation, refs=refs, first_group=first_group, n_blocks=n_blocks):
            work = []
            for j in range(ATTN_UNROLL):
                f = it * ATTN_UNROLL + j
                r = lax.shift_right_logical(f, n_blocks.bit_length() - 1)
                n = lax.bitwise_and(f, n_blocks - 1)
                kind = jnp.where(n == 0, jnp.where(f == 0, ROW_ZERO, NO_PREV), WITH_PREV)
                start = n * (blk * dilation) + r
                rows = (pl.ds(_aligned(start, blk), blk) if dilation == 1
                        else pl.ds(start, blk, stride=dilation))
                work.append((scores(g * N_BIAS_KINDS + kind, *refs, f * blk, jnp.maximum(f - 1, 0) * blk),
                             rows))
            for vals, rows in work:
                merge(rows, first_group, *vals)
            return carry

        lax.fori_loop(0, total_blocks // ATTN_UNROLL, step, 0)

    o_ref[0] = (acc_ref[...] / l_ref[...]).astype(BF16)


def _attn_core(qkv_groups, bias):
    b, _, n_pairs, s, _ = qkv_groups[0].shape
    in_specs, args = [], []
    for qkv in qkv_groups:
        for i in range(3):
            in_specs.append(pl.BlockSpec((None, None, None, s, LANES),
                                         lambda bi, hp, i=i: (bi, i, hp, 0, 0)))
            args.append(qkv)
    n_kinds = bias.shape[0]
    in_specs.append(pl.BlockSpec((n_kinds, None, 2, ATTN_BLOCK, 2 * ATTN_BLOCK),
                                 lambda bi, hp: (0, hp, 0, 0, 0)))
    nbytes = (2 * 9 * s * LANES * 2 + 2 * n_kinds * 2 * ATTN_BLOCK * 2 * ATTN_BLOCK * 4
              + 2 * s * LANES * 2 + 3 * s * LANES * 4)
    return pl.pallas_call(
        functools.partial(_attn_kernel, seq=s),
        out_shape=jax.ShapeDtypeStruct((b, s, n_pairs * LANES), BF16),
        grid=(b, n_pairs),
        in_specs=in_specs,
        out_specs=pl.BlockSpec((1, s, LANES), lambda bi, hp: (bi, 0, hp)),
        scratch_shapes=[pltpu.VMEM((s, LANES), F32)] * 3,
        compiler_params=pltpu.CompilerParams(
            dimension_semantics=("arbitrary", "arbitrary"),
            vmem_limit_bytes=_vmem_limit(nbytes)),
        name="dilated_attn",
    )(*args, bias)


def _attn_mixer(x, g_in, w_qkv):
    n_pairs = w_qkv.shape[1] // (len(DILATED_CFG) * 3 * LANES)
    assert n_pairs * 2 == HEADS_PER_GROUP and 2 * HEAD_DIM == LANES
    qkv_groups = _qkv_proj(x, g_in, w_qkv.astype(BF16), n_pairs)
    return _attn_core(qkv_groups, _attn_bias_table(n_pairs))


def kernel(x, norm_g, attn_w_qkv, attn_w_o, conv_w_in, conv_w_dw, conv_w_out,
           pool_w_in, pool_w_grp, pool_scale, pool_w_out, ffn_w_up, ffn_w_dw, ffn_w_down):
    depth = norm_g.shape[0]
    ia = ib = ic = 0
    for i in range(depth):
        g = norm_g[i]
        kind = i % 3
        if kind == 0:
            y, w_mix = _attn_mixer(x, g[0], attn_w_qkv[ia]), attn_w_o[ia]
            ia += 1
        elif kind == 1:
            y, w_mix = _conv_mixer(x, g[0], conv_w_in[ib], conv_w_dw[ib]), conv_w_out[ib]
            ib += 1
        else:
            y, w_mix = _pool_mixer(x, g[0], pool_w_in[ic], pool_w_grp[ic], pool_scale[ic]), pool_w_out[ic]
            ic += 1
        x = _layer_tail(x, y, w_mix, g[1], g[2], g[3], ffn_w_up[i], ffn_w_dw[i], ffn_w_down[i])
    return x
```

```python
import functools
import math

import numpy as np
import jax
import jax.numpy as jnp
from jax import lax
from jax.experimental import pallas as pl
from jax.experimental.pallas import tpu as pltpu

F32 = jnp.float32
BF16 = jnp.bfloat16

RMS_EPS = 1e-6
NEG_INF = -1e30

DILATED_CFG = ((128, 1), (512, 4), (2048, 16))
HEADS_PER_GROUP = 8
HEAD_DIM = 64
LOG2_E = 1.4426950408889634
Q_SCALE = HEAD_DIM ** -0.5 * LOG2_E
ATTN_BLOCK = 128
ATTN_UNROLL = 16
POOL_WINDOWS = (2, 4, 8, 16)

LANES = 128
BF16_SUBLANES = 16
MXU_DIM = 256
VMEM_CAP_BYTES = 60000 * 1024

ROW_TILE = 512
BIG_ROW_TILE = 1024
CONV_HALO = 8
POOL_HALO = 16
WEIGHT_CHUNK_BYTES = 3 << 19


def _vmem_limit(nbytes):
    return int(min(VMEM_CAP_BYTES, nbytes + (16 << 20)))


def _rms(x, g):
    ms = jnp.mean(x * x, axis=-1, keepdims=True)
    return x * lax.rsqrt(ms + RMS_EPS) * g


def _aligned(v, m):
    return v if isinstance(v, int) else pl.multiple_of(v, m)


def _resident(shape):
    nd = len(shape)
    return pl.BlockSpec(shape, lambda *_: (0,) * nd, pipeline_mode=pl.Buffered(1))


def _row_tiled(rows, width):
    return pl.BlockSpec((1, rows, width), lambda bi, i: (bi, i, 0))


_IN_HBM = pl.BlockSpec(memory_space=pl.ANY)


def _weight_staging(cols, *row_counts):
    common = functools.reduce(math.gcd, row_counts)
    rows = max(r for r in range(BF16_SUBLANES, common + 1, BF16_SUBLANES)
               if common % r == 0 and r * cols * 4 <= WEIGHT_CHUNK_BYTES)
    return [pltpu.VMEM((2, rows, cols), F32), pltpu.SemaphoreType.DMA((2,))]


def _load_weights_bf16(jobs):
    plan, used = [], {}
    for src, dst, stage, sems in jobs:
        rows = stage.shape[1]
        assert src.shape[0] % rows == 0 and src.shape[1] == stage.shape[2], (src.shape, stage.shape)
        for r0 in range(0, src.shape[0], rows):
            slot = used.get(id(stage), 0) % 2
            used[id(stage)] = used.get(id(stage), 0) + 1
            plan.append((src, dst, stage, sems, r0, rows, slot))

    def copy(i):
        src, _, stage, sems, r0, rows, slot = plan[i]
        return pltpu.make_async_copy(src.at[pl.ds(r0, rows), :], stage.at[slot], sems.at[slot])

    @pl.when(jnp.logical_and(pl.program_id(0) == 0, pl.program_id(1) == 0))
    def _():
        copy(0).start()
        for i, (_, dst, stage, _, r0, rows, slot) in enumerate(plan):
            if i + 1 < len(plan):
                copy(i + 1).start()
            copy(i).wait()
            dst[r0:r0 + rows, :] = stage[slot].astype(BF16)


def _causal_conv3(cur, halo_ref, carry_ref, w, rows):
    out = []
    for s in range(cur.shape[1] // LANES):
        cols = slice(s * LANES, (s + 1) * LANES)
        piece = cur[:, cols]
        halo_ref[s, 0:CONV_HALO, :] = carry_ref[:, cols]
        halo_ref[s, CONV_HALO:CONV_HALO + rows, :] = piece
        carry_ref[:, cols] = piece[rows - CONV_HALO:rows, :]
        out.append(piece * w[2:3, cols]
                   + halo_ref[s, CONV_HALO - 1:CONV_HALO - 1 + rows, :] * w[1:2, cols]
                   + halo_ref[s, CONV_HALO - 2:CONV_HALO - 2 + rows, :] * w[0:1, cols])
    return jnp.concatenate(out, axis=1)


def _zero_at_sequence_start(carry_ref):
    @pl.when(pl.program_id(1) == 0)
    def _():
        carry_ref[...] = jnp.zeros(carry_ref.shape, carry_ref.dtype)


def _layer_tail_kernel(x_ref, y_ref, wmix_hbm, gmix_ref, gin_ref, wup_hbm, wdw_ref, wdown_hbm, gout_ref,
                       o_ref, halo_ref, carry_ref, act_ref, wmix_ref, wup_ref, wdown_ref,
                       stage_wide, sem_wide, stage, sem, *, rows, sub_rows, d_ff, chunk, mix_layer, ffn_layer):
    _load_weights_bf16([(wmix_hbm.at[mix_layer], wmix_ref, stage, sem),
                        (wup_hbm.at[ffn_layer], wup_ref, stage_wide, sem_wide),
                        (wdown_hbm.at[ffn_layer], wdown_ref, stage, sem)])
    _zero_at_sequence_start(carry_ref)
    for t in range(rows // sub_rows):
        tile = slice(t * sub_rows, (t + 1) * sub_rows)
        mixed = jnp.dot(y_ref[0, tile, :], wmix_ref[...], preferred_element_type=F32)
        x = x_ref[0, tile, :] + _rms(mixed, gmix_ref[...])
        xn = _rms(x, gin_ref[...]).astype(BF16)
        for c in range(d_ff // chunk):
            halves = []
            for part in range(2):
                cols = slice(part * d_ff + c * chunk, part * d_ff + (c + 1) * chunk)
                h = jnp.dot(xn, wup_ref[:, cols], preferred_element_type=F32)
                slot = 2 * c + part
                halves.append(_causal_conv3(h, halo_ref.at[slot % 4], carry_ref.at[slot],
                                            wdw_ref[:, cols], sub_rows))
            gate, lin = halves
            act_ref[tile, c * chunk:(c + 1) * chunk] = (gate * jax.nn.sigmoid(gate) * lin).astype(BF16)
        out = jnp.dot(act_ref[tile, :], wdown_ref[...], preferred_element_type=F32)
        o_ref[0, tile, :] = x + _rms(out, gout_ref[...])


def _layer_tail(x, y, w_mix_stack, mix_layer, g_mix, g_in, g_out, w_up_stack, w_dw, w_down_stack, ffn_layer):
    b, s, d = x.shape
    mix_shape, up_shape, down_shape = w_mix_stack.shape[1:], w_up_stack.shape[1:], w_down_stack.shape[1:]
    d_ff = down_shape[0]
    d_mix = y.shape[-1]
    chunk = MXU_DIM
    rows, sub_rows = BIG_ROW_TILE, ROW_TILE
    assert d_ff % chunk == 0 and s % rows == 0 and rows % sub_rows == 0 and mix_shape[0] == d_mix
    n_slots = 2 * (d_ff // chunk)
    staging = _weight_staging(up_shape[1], up_shape[0]) + _weight_staging(d, mix_shape[0], d_ff)
    nbytes = (2 * (math.prod(up_shape) + math.prod(down_shape) + math.prod(mix_shape)) + 8 * w_dw.shape[1] * 4
              + 4 * rows * d * 4 + 2 * rows * d_mix * 2
              + 4 * (CONV_HALO + sub_rows) * chunk * 4 + n_slots * CONV_HALO * chunk * 4
              + rows * d_ff * 2 + 4 * WEIGHT_CHUNK_BYTES)
    return pl.pallas_call(
        functools.partial(_layer_tail_kernel, rows=rows, sub_rows=sub_rows, d_ff=d_ff, chunk=chunk,
                          mix_layer=mix_layer, ffn_layer=ffn_layer),
        out_shape=jax.ShapeDtypeStruct(x.shape, F32),
        grid=(b, s // rows),
        in_specs=[
            _row_tiled(rows, d),
            _row_tiled(rows, d_mix),
            _IN_HBM,
            _resident((1, d)),
            _resident((1, d)),
            _IN_HBM,
            _resident(w_dw.shape),
            _IN_HBM,
            _resident((1, d)),
        ],
        out_specs=_row_tiled(rows, d),
        scratch_shapes=[
            pltpu.VMEM((4, chunk // LANES, CONV_HALO + sub_rows, LANES), F32),
            pltpu.VMEM((n_slots, CONV_HALO, chunk), F32),
            pltpu.VMEM((rows, d_ff), BF16),
            pltpu.VMEM(mix_shape, BF16),
            pltpu.VMEM(up_shape, BF16),
            pltpu.VMEM(down_shape, BF16),
            *staging,
        ],
        compiler_params=pltpu.CompilerParams(
            dimension_semantics=("arbitrary", "arbitrary"),
            vmem_limit_bytes=_vmem_limit(nbytes)),
        name="layer_tail",
    )(x, y, w_mix_stack, g_mix.reshape(1, d), g_in.reshape(1, d), w_up_stack, w_dw, w_down_stack,
      g_out.reshape(1, d))


def _conv_mixer_kernel(x_ref, gin_ref, win_hbm, wdw_ref, o_ref, halo_ref, carry_ref, win_ref, stage, sem,
                       *, rows, sub_rows, d, chunk, layer):
    _load_weights_bf16([(win_hbm.at[layer], win_ref, stage, sem)])
    _zero_at_sequence_start(carry_ref)
    for t in range(rows // sub_rows):
        tile = slice(t * sub_rows, (t + 1) * sub_rows)
        xn = _rms(x_ref[0, tile, :], gin_ref[...]).astype(BF16)
        for c in range(d // chunk):
            cols = slice(c * chunk, (c + 1) * chunk)
            gate_b, gate_c, h = [
                jnp.dot(xn, win_ref[:, part * d + c * chunk:part * d + (c + 1) * chunk],
                        preferred_element_type=F32) for part in range(3)]
            conv = _causal_conv3(gate_c * h, halo_ref.at[c % 2], carry_ref.at[c], wdw_ref[:, cols],
                                 sub_rows)
            o_ref[0, tile, cols] = (gate_b * conv).astype(BF16)


def _conv_mixer(x, g_in, w_in_stack, layer, w_dw):
    b, s, d = x.shape
    in_shape = w_in_stack.shape[1:]
    chunk = MXU_DIM
    rows, sub_rows = BIG_ROW_TILE, ROW_TILE
    n_chunks = d // chunk
    nbytes = (math.prod(in_shape) * 2 + 8 * d * 4 + 2 * rows * d * 4 + 2 * rows * d * 2
              + 2 * (CONV_HALO + sub_rows) * chunk * 4 + n_chunks * CONV_HALO * chunk * 4
              + 2 * WEIGHT_CHUNK_BYTES)
    return pl.pallas_call(
        functools.partial(_conv_mixer_kernel, rows=rows, sub_rows=sub_rows, d=d, chunk=chunk, layer=layer),
        out_shape=jax.ShapeDtypeStruct(x.shape, BF16),
        grid=(b, s // rows),
        in_specs=[
            _row_tiled(rows, d),
            _resident((1, d)),
            _IN_HBM,
            _resident(w_dw.shape),
        ],
        out_specs=_row_tiled(rows, d),
        scratch_shapes=[
            pltpu.VMEM((2, chunk // LANES, CONV_HALO + sub_rows, LANES), F32),
            pltpu.VMEM((n_chunks, CONV_HALO, chunk), F32),
            pltpu.VMEM(in_shape, BF16),
            *_weight_staging(in_shape[1], in_shape[0]),
        ],
        compiler_params=pltpu.CompilerParams(
            dimension_semantics=("arbitrary", "arbitrary"),
            vmem_limit_bytes=_vmem_limit(nbytes)),
        name="short_conv_mixer",
    )(x, g_in.reshape(1, d), w_in_stack, w_dw)


def _pool_mixer_kernel(x_ref, gin_ref, win_ref, wgrp_ref, scale_ref, o_ref, halo_ref, carry_ref,
                       *, rows, sub_rows, group_dim):
    _zero_at_sequence_start(carry_ref)
    slabs_per_group = group_dim // LANES
    for t in range(rows // sub_rows):
        tile = slice(t * sub_rows, (t + 1) * sub_rows)
        xn = _rms(x_ref[0, tile, :], gin_ref[...]).astype(BF16)
        u = jnp.dot(xn, win_ref[...], preferred_element_type=F32)
        pos = (pl.program_id(1) * rows + t * sub_rows + 1
               + lax.broadcasted_iota(jnp.int32, (sub_rows, LANES), 0))
        for g, w in enumerate(POOL_WINDOWS):
            inv_count = 1.0 / jnp.minimum(pos, w).astype(F32)
            pooled = []
            for s in range(g * slabs_per_group, (g + 1) * slabs_per_group):
                cols = slice(s * LANES, (s + 1) * LANES)
                us = u[:, cols]
                halo = halo_ref.at[t % 2, s]
                halo[0:POOL_HALO, :] = carry_ref[:, cols]
                halo[POOL_HALO:POOL_HALO + sub_rows, :] = us
                carry_ref[:, cols] = us[sub_rows - POOL_HALO:sub_rows, :]
                total = us
                for k in range(1, w):
                    total = total + halo[POOL_HALO - k:POOL_HALO - k + sub_rows, :]
                pooled.append((total * inv_count - us).astype(BF16))
            cols = slice(g * group_dim, (g + 1) * group_dim)
            yg = jnp.dot(jnp.concatenate(pooled, axis=1), wgrp_ref[g], preferred_element_type=F32)
            o_ref[0, tile, cols] = (yg * scale_ref[:, cols]).astype(BF16)


def _pool_mixer(x, g_in, w_in, w_grp, scale):
    b, s, d = x.shape
    n_groups, group_dim, _ = w_grp.shape
    assert n_groups == len(POOL_WINDOWS) and n_groups * group_dim == d
    rows, sub_rows = BIG_ROW_TILE, ROW_TILE
    nbytes = (w_in.size * 2 + w_grp.size * 2 + 2 * rows * d * 4 + 2 * rows * d * 2
              + 2 * (POOL_HALO + sub_rows) * d * 4 + POOL_HALO * d * 4)
    return pl.pallas_call(
        functools.partial(_pool_mixer_kernel, rows=rows, sub_rows=sub_rows, group_dim=group_dim),
        out_shape=jax.ShapeDtypeStruct(x.shape, BF16),
        grid=(b, s // rows),
        in_specs=[
            _row_tiled(rows, d),
            _resident((1, d)),
            _resident(w_in.shape),
            _resident(w_grp.shape),
            _resident((1, d)),
        ],
        out_specs=_row_tiled(rows, d),
        scratch_shapes=[
            pltpu.VMEM((2, d // LANES, POOL_HALO + sub_rows, LANES), F32),
            pltpu.VMEM((POOL_HALO, d), F32),
        ],
        compiler_params=pltpu.CompilerParams(
            dimension_semantics=("arbitrary", "arbitrary"),
            vmem_limit_bytes=_vmem_limit(nbytes)),
        name="pool_mixer",
    )(x, g_in.reshape(1, d), w_in.astype(BF16), w_grp.astype(BF16), scale.reshape(1, d))


def _qkv_kernel(x_ref, gin_ref, w_hbm, *refs, rows, sub_rows, d, n_pairs, layer):
    n_groups = len(DILATED_CFG)
    outs, (slab_ref, perm_ref, w_ref, stage, sem) = refs[:n_groups], refs[n_groups:]
    _load_weights_bf16([(w_hbm.at[layer], w_ref, stage, sem)])
    width = 3 * n_pairs * LANES
    n_slabs = d // LANES
    for t in range(rows // sub_rows):
        xn = _rms(x_ref[0, t * sub_rows:(t + 1) * sub_rows, :], gin_ref[...])
        slab = slab_ref.at[t % 2]
        for s in range(n_slabs):
            slab[s] = xn[:, s * LANES:(s + 1) * LANES]
        for g, (_, dilation) in enumerate(DILATED_CFG):
            sub = sub_rows // dilation
            if dilation == 1:
                xg = xn.astype(BF16)
            else:
                perm = perm_ref.at[t % 2, g % 2]
                for r in range(dilation):
                    for s in range(n_slabs):
                        perm[r * sub:(r + 1) * sub, s * LANES:(s + 1) * LANES] = (
                            slab[s, pl.ds(r, sub, stride=dilation), :].astype(BF16))
                xg = perm[...]
            y = jnp.dot(xg, w_ref[:, g * width:(g + 1) * width], preferred_element_type=F32)
            for i in range(3):
                for hp in range(n_pairs):
                    lo = (i * n_pairs + hp) * LANES
                    part = y[:, lo:lo + LANES]
                    if i == 0:
                        part = part * Q_SCALE
                    for r in range(dilation):
                        outs[g][0, i, hp, r, t * sub:(t + 1) * sub, :] = (
                            part[r * sub:(r + 1) * sub, :].astype(BF16))


def _qkv_proj(x, g_in, w_qkv_stack, layer, n_pairs):
    b, s, d = x.shape
    w_shape = w_qkv_stack.shape[1:]
    rows, sub_rows = BIG_ROW_TILE, ROW_TILE
    out_shapes, out_specs = [], []
    for _, dilation in DILATED_CFG:
        length = s // dilation
        out_shapes.append(jax.ShapeDtypeStruct((b, 3, n_pairs, dilation, length, LANES), BF16))
        out_specs.append(pl.BlockSpec((1, 3, n_pairs, dilation, rows // dilation, LANES),
                                      lambda bi, j: (bi, 0, 0, 0, j, 0)))
    nbytes = (math.prod(w_shape) * 2 + 2 * rows * d * 4 + 2 * rows * w_shape[1] * 2
              + 2 * sub_rows * d * 4 + 4 * sub_rows * d * 2 + 2 * WEIGHT_CHUNK_BYTES)
    outs = pl.pallas_call(
        functools.partial(_qkv_kernel, rows=rows, sub_rows=sub_rows, d=d, n_pairs=n_pairs, layer=layer),
        out_shape=out_shapes,
        grid=(b, s // rows),
        in_specs=[_row_tiled(rows, d), _resident((1, d)), _IN_HBM],
        out_specs=out_specs,
        scratch_shapes=[
            pltpu.VMEM((2, d // LANES, sub_rows, LANES), F32),
            pltpu.VMEM((2, 2, sub_rows, d), BF16),
            pltpu.VMEM(w_shape, BF16),
            *_weight_staging(w_shape[1], w_shape[0]),
        ],
        compiler_params=pltpu.CompilerParams(
            dimension_semantics=("arbitrary", "arbitrary"),
            vmem_limit_bytes=_vmem_limit(nbytes)),
        name="qkv_proj",
    )(x, g_in.reshape(1, d), w_qkv_stack)
    return [o.reshape(b, 3, n_pairs, s, LANES) for o in outs]


def _attn_bias_table(n_pairs):
    n_heads = len(DILATED_CFG) * HEADS_PER_GROUP
    slopes = (2.0 ** (-8.0 * np.arange(1, n_heads + 1) / n_heads)).astype(np.float32)
    delta = ATTN_BLOCK + np.arange(ATTN_BLOCK)[:, None] - np.arange(2 * ATTN_BLOCK)[None, :]
    neg = np.float32(NEG_INF)
    tables = []
    for g, (window, dilation) in enumerate(DILATED_CFG):
        span = window // dilation
        valid = (delta >= 0) & (delta <= span)
        dist = (delta * dilation).astype(np.float32)
        with_prev = np.stack([np.where(valid, -slopes[g * HEADS_PER_GROUP + h] * dist * np.float32(LOG2_E), neg)
                              for h in range(HEADS_PER_GROUP)])
        with_prev = with_prev.reshape(n_pairs, 2, ATTN_BLOCK, 2 * ATTN_BLOCK)
        no_prev = with_prev.copy()
        no_prev[..., :ATTN_BLOCK] = neg
        row_zero = np.full_like(with_prev, neg)
        row_zero[..., :ATTN_BLOCK] = with_prev[..., ATTN_BLOCK:]
        tables += [with_prev, no_prev, row_zero]
    return jnp.asarray(np.stack(tables), F32)


WITH_PREV, NO_PREV, ROW_ZERO = 0, 1, 2
N_BIAS_KINDS = 3


def _attn_kernel(q0, k0, v0, q1, k1, v1, q2, k2, v2, bias_ref, o_ref, acc_ref, m_ref, l_ref, *, seq):
    blk = ATTN_BLOCK
    low_half = lax.broadcasted_iota(jnp.int32, (blk, LANES), 1) < HEAD_DIM

    def scores(kind, q_ref, k_ref, v_ref, q_start, k_start):
        q = q_ref[pl.ds(_aligned(q_start, blk), blk), :]
        k = k_ref[pl.ds(_aligned(k_start, blk), 2 * blk), :]
        v = v_ref[pl.ds(_aligned(k_start, blk), 2 * blk), :]
        v_ext = jnp.concatenate([v, jnp.ones((2 * blk, LANES), BF16)], axis=1)
        m_h, l_h, pv_h = [], [], []
        for a in range(2):
            qa = jnp.where(low_half if a == 0 else ~low_half, q, jnp.zeros_like(q))
            s = lax.dot_general(qa, k, (((1,), (1,)), ((), ())), preferred_element_type=F32)
            s = s + bias_ref[kind, a]
            m = jnp.max(s, axis=-1, keepdims=True)
            p = jnp.exp2(s - m).astype(BF16)
            pvl = jnp.dot(p, v_ext, preferred_element_type=F32)
            m_h.append(m)
            pv_h.append(pvl[:, :LANES])
            l_h.append(pvl[:, LANES:])
        return (jnp.where(low_half, m_h[0], m_h[1]), jnp.where(low_half, l_h[0], l_h[1]),
                jnp.where(low_half, pv_h[0], pv_h[1]))

    def merge(out_rows, first_group, m_b, l_b, pv_b):
        if first_group:
            m_ref[out_rows, :] = m_b
            l_ref[out_rows, :] = l_b
            acc_ref[out_rows, :] = pv_b
        else:
            m_old = m_ref[out_rows, :]
            m_new = jnp.maximum(m_old, m_b)
            a_old = jnp.exp2(m_old - m_new)
            a_blk = jnp.exp2(m_b - m_new)
            m_ref[out_rows, :] = m_new
            l_ref[out_rows, :] = l_ref[out_rows, :] * a_old + l_b * a_blk
            acc_ref[out_rows, :] = acc_ref[out_rows, :] * a_old + pv_b * a_blk

    groups = ((q0, k0, v0), (q1, k1, v1), (q2, k2, v2))
    total_blocks = seq // blk
    order = sorted(range(len(DILATED_CFG)), key=lambda g: -DILATED_CFG[g][1])
    for g in order:
        dilation, refs, first_group = DILATED_CFG[g][1], groups[g], g == order[0]
        n_blocks = total_blocks // dilation
        assert n_blocks & (n_blocks - 1) == 0 and total_blocks % ATTN_UNROLL == 0

        def step(it, carry, g=g, dilation=dilation, refs=refs, first_group=first_group, n_blocks=n_blocks):
            work = []
            for j in range(ATTN_UNROLL):
                f = it * ATTN_UNROLL + j
                r = lax.shift_right_logical(f, n_blocks.bit_length() - 1)
                n = lax.bitwise_and(f, n_blocks - 1)
                kind = jnp.where(n == 0, jnp.where(f == 0, ROW_ZERO, NO_PREV), WITH_PREV)
                start = n * (blk * dilation) + r
                rows = (pl.ds(_aligned(start, blk), blk) if dilation == 1
                        else pl.ds(start, blk, stride=dilation))
                work.append((scores(g * N_BIAS_KINDS + kind, *refs, f * blk, jnp.maximum(f - 1, 0) * blk),
                             rows))
            for vals, rows in work:
                merge(rows, first_group, *vals)
            return carry

        lax.fori_loop(0, total_blocks // ATTN_UNROLL, step, 0)

    o_ref[0] = (acc_ref[...] / l_ref[...]).astype(BF16)


def _attn_core(qkv_groups, bias):
    b, _, n_pairs, s, _ = qkv_groups[0].shape
    in_specs, args = [], []
    for qkv in qkv_groups:
        for i in range(3):
            in_specs.append(pl.BlockSpec((None, None, None, s, LANES),
                                         lambda bi, hp, i=i: (bi, i, hp, 0, 0)))
            args.append(qkv)
    n_kinds = bias.shape[0]
    in_specs.append(pl.BlockSpec((n_kinds, None, 2, ATTN_BLOCK, 2 * ATTN_BLOCK),
                                 lambda bi, hp: (0, hp, 0, 0, 0)))
    nbytes = (2 * 9 * s * LANES * 2 + 2 * n_kinds * 2 * ATTN_BLOCK * 2 * ATTN_BLOCK * 4
              + 2 * s * LANES * 2 + 3 * s * LANES * 4)
    return pl.pallas_call(
        functools.partial(_attn_kernel, seq=s),
        out_shape=jax.ShapeDtypeStruct((b, s, n_pairs * LANES), BF16),
        grid=(b, n_pairs),
        in_specs=in_specs,
        out_specs=pl.BlockSpec((1, s, LANES), lambda bi, hp: (bi, 0, hp)),
        scratch_shapes=[pltpu.VMEM((s, LANES), F32)] * 3,
        compiler_params=pltpu.CompilerParams(
            dimension_semantics=("arbitrary", "arbitrary"),
            vmem_limit_bytes=_vmem_limit(nbytes)),
        name="dilated_attn",
    )(*args, bias)


def _attn_mixer(x, g_in, w_qkv_stack, layer):
    n_pairs = w_qkv_stack.shape[2] // (len(DILATED_CFG) * 3 * LANES)
    assert n_pairs * 2 == HEADS_PER_GROUP and 2 * HEAD_DIM == LANES
    qkv_groups = _qkv_proj(x, g_in, w_qkv_stack, layer, n_pairs)
    return _attn_core(qkv_groups, _attn_bias_table(n_pairs))


def kernel(x, norm_g, attn_w_qkv, attn_w_o, conv_w_in, conv_w_dw, conv_w_out,
           pool_w_in, pool_w_grp, pool_scale, pool_w_out, ffn_w_up, ffn_w_dw, ffn_w_down):
    depth = norm_g.shape[0]
    ia = ib = ic = 0
    for i in range(depth):
        g = norm_g[i]
        kind = i % 3
        if kind == 0:
            y, w_mix, mix_layer = _attn_mixer(x, g[0], attn_w_qkv, ia), attn_w_o, ia
            ia += 1
        elif kind == 1:
            y, w_mix, mix_layer = _conv_mixer(x, g[0], conv_w_in, ib, conv_w_dw[ib]), conv_w_out, ib
            ib += 1
        else:
            y = _pool_mixer(x, g[0], pool_w_in[ic], pool_w_grp[ic], pool_scale[ic])
            w_mix, mix_layer = pool_w_out, ic
            ic += 1
        x = _layer_tail(x, y, w_mix, mix_layer, g[1], g[2], g[3], ffn_w_up, ffn_w_dw[i], ffn_w_down, i)
    return x
```

```python
import functools
import math

import numpy as np
import jax
import jax.numpy as jnp
from jax import lax
from jax.experimental import pallas as pl
from jax.experimental.pallas import tpu as pltpu

F32 = jnp.float32
BF16 = jnp.bfloat16

RMS_EPS = 1e-6
NEG_INF = -1e30

DILATED_CFG = ((128, 1), (512, 4), (2048, 16))
HEADS_PER_GROUP = 8
HEAD_DIM = 64
LOG2_E = 1.4426950408889634
Q_SCALE = HEAD_DIM ** -0.5 * LOG2_E
ATTN_BLOCK = 128
ATTN_UNROLL = 16
POOL_WINDOWS = (2, 4, 8, 16)

LANES = 128
BF16_SUBLANES = 16
MXU_DIM = 256
VMEM_CAP_BYTES = 60000 * 1024

ROW_TILE = 512
BIG_ROW_TILE = 1024
CONV_HALO = 8
POOL_HALO = 16
WEIGHT_CHUNK_BYTES = 3 << 18
WEIGHT_SLOTS = 4


def _vmem_limit(nbytes):
    return int(min(VMEM_CAP_BYTES, nbytes + (16 << 20)))


def _rms(x, g):
    ms = jnp.mean(x * x, axis=-1, keepdims=True)
    return x * lax.rsqrt(ms + RMS_EPS) * g


def _aligned(v, m):
    return v if isinstance(v, int) else pl.multiple_of(v, m)


def _resident(shape):
    nd = len(shape)
    return pl.BlockSpec(shape, lambda *_: (0,) * nd, pipeline_mode=pl.Buffered(1))


def _row_tiled(rows, width):
    return pl.BlockSpec((1, rows, width), lambda bi, i: (bi, i, 0))


_IN_HBM = pl.BlockSpec(memory_space=pl.ANY)


def _weight_staging(cols, *row_counts):
    common = functools.reduce(math.gcd, row_counts)
    rows = max(r for r in range(BF16_SUBLANES, common + 1, BF16_SUBLANES)
               if common % r == 0 and r * cols * 4 <= WEIGHT_CHUNK_BYTES)
    return [pltpu.VMEM((WEIGHT_SLOTS, rows, cols), F32), pltpu.SemaphoreType.DMA((WEIGHT_SLOTS,))]


def _load_weights_bf16(jobs):
    plan, used = [], {}
    for src, dst, stage, sems in jobs:
        rows = stage.shape[1]
        assert src.shape[0] % rows == 0 and src.shape[1] == stage.shape[2], (src.shape, stage.shape)
        for r0 in range(0, src.shape[0], rows):
            slot = used.get(id(stage), 0) % WEIGHT_SLOTS
            used[id(stage)] = used.get(id(stage), 0) + 1
            plan.append((src, dst, stage, sems, r0, rows, slot))

    def copy(i):
        src, _, stage, sems, r0, rows, slot = plan[i]
        return pltpu.make_async_copy(src.at[pl.ds(r0, rows), :], stage.at[slot], sems.at[slot])

    ahead = WEIGHT_SLOTS - 1

    @pl.when(jnp.logical_and(pl.program_id(0) == 0, pl.program_id(1) == 0))
    def _():
        for i in range(min(ahead, len(plan))):
            copy(i).start()
        for i, (_, dst, stage, _, r0, rows, slot) in enumerate(plan):
            if i + ahead < len(plan):
                copy(i + ahead).start()
            copy(i).wait()
            dst[r0:r0 + rows, :] = stage[slot].astype(BF16)


def _causal_conv3(cur, halo_ref, carry_ref, w, rows):
    out = []
    for s in range(cur.shape[1] // LANES):
        cols = slice(s * LANES, (s + 1) * LANES)
        piece = cur[:, cols]
        halo_ref[s, 0:CONV_HALO, :] = carry_ref[:, cols]
        halo_ref[s, CONV_HALO:CONV_HALO + rows, :] = piece
        carry_ref[:, cols] = piece[rows - CONV_HALO:rows, :]
        out.append(piece * w[2:3, cols]
                   + halo_ref[s, CONV_HALO - 1:CONV_HALO - 1 + rows, :] * w[1:2, cols]
                   + halo_ref[s, CONV_HALO - 2:CONV_HALO - 2 + rows, :] * w[0:1, cols])
    return jnp.concatenate(out, axis=1)


def _zero_at_sequence_start(carry_ref):
    @pl.when(pl.program_id(1) == 0)
    def _():
        carry_ref[...] = jnp.zeros(carry_ref.shape, carry_ref.dtype)


def _layer_tail_kernel(x_ref, y_ref, wmix_hbm, gmix_ref, gin_ref, wup_hbm, wdw_ref, wdown_hbm, gout_ref,
                       o_ref, halo_ref, carry_ref, act_ref, wmix_ref, wup_ref, wdown_ref,
                       stage_wide, sem_wide, stage, sem, *, rows, sub_rows, d_ff, chunk, mix_layer, ffn_layer):
    _load_weights_bf16([(wmix_hbm.at[mix_layer], wmix_ref, stage, sem),
                        (wup_hbm.at[ffn_layer], wup_ref, stage_wide, sem_wide),
                        (wdown_hbm.at[ffn_layer], wdown_ref, stage, sem)])
    _zero_at_sequence_start(carry_ref)
    for t in range(rows // sub_rows):
        tile = slice(t * sub_rows, (t + 1) * sub_rows)
        mixed = jnp.dot(y_ref[0, tile, :], wmix_ref[...], preferred_element_type=F32)
        x = x_ref[0, tile, :] + _rms(mixed, gmix_ref[...])
        xn = _rms(x, gin_ref[...]).astype(BF16)
        for c in range(d_ff // chunk):
            halves = []
            for part in range(2):
                cols = slice(part * d_ff + c * chunk, part * d_ff + (c + 1) * chunk)
                h = jnp.dot(xn, wup_ref[:, cols], preferred_element_type=F32)
                slot = 2 * c + part
                halves.append(_causal_conv3(h, halo_ref.at[slot % 4], carry_ref.at[slot],
                                            wdw_ref[:, cols], sub_rows))
            gate, lin = halves
            act_ref[tile, c * chunk:(c + 1) * chunk] = (gate * jax.nn.sigmoid(gate) * lin).astype(BF16)
        out = jnp.dot(act_ref[tile, :], wdown_ref[...], preferred_element_type=F32)
        o_ref[0, tile, :] = x + _rms(out, gout_ref[...])


def _layer_tail(x, y, w_mix_stack, mix_layer, g_mix, g_in, g_out, w_up_stack, w_dw, w_down_stack, ffn_layer):
    b, s, d = x.shape
    mix_shape, up_shape, down_shape = w_mix_stack.shape[1:], w_up_stack.shape[1:], w_down_stack.shape[1:]
    d_ff = down_shape[0]
    d_mix = y.shape[-1]
    chunk = MXU_DIM
    rows, sub_rows = BIG_ROW_TILE, ROW_TILE
    assert d_ff % chunk == 0 and s % rows == 0 and rows % sub_rows == 0 and mix_shape[0] == d_mix
    n_slots = 2 * (d_ff // chunk)
    staging = _weight_staging(up_shape[1], up_shape[0]) + _weight_staging(d, mix_shape[0], d_ff)
    nbytes = (2 * (math.prod(up_shape) + math.prod(down_shape) + math.prod(mix_shape)) + 8 * w_dw.shape[1] * 4
              + 4 * rows * d * 4 + 2 * rows * d_mix * 2
              + 4 * (CONV_HALO + sub_rows) * chunk * 4 + n_slots * CONV_HALO * chunk * 4
              + rows * d_ff * 2 + 2 * WEIGHT_SLOTS * WEIGHT_CHUNK_BYTES)
    return pl.pallas_call(
        functools.partial(_layer_tail_kernel, rows=rows, sub_rows=sub_rows, d_ff=d_ff, chunk=chunk,
                          mix_layer=mix_layer, ffn_layer=ffn_layer),
        out_shape=jax.ShapeDtypeStruct(x.shape, F32),
        grid=(b, s // rows),
        in_specs=[
            _row_tiled(rows, d),
            _row_tiled(rows, d_mix),
            _IN_HBM,
            _resident((1, d)),
            _resident((1, d)),
            _IN_HBM,
            _resident(w_dw.shape),
            _IN_HBM,
            _resident((1, d)),
        ],
        out_specs=_row_tiled(rows, d),
        scratch_shapes=[
            pltpu.VMEM((4, chunk // LANES, CONV_HALO + sub_rows, LANES), F32),
            pltpu.VMEM((n_slots, CONV_HALO, chunk), F32),
            pltpu.VMEM((rows, d_ff), BF16),
            pltpu.VMEM(mix_shape, BF16),
            pltpu.VMEM(up_shape, BF16),
            pltpu.VMEM(down_shape, BF16),
            *staging,
        ],
        compiler_params=pltpu.CompilerParams(
            dimension_semantics=("arbitrary", "arbitrary"),
            vmem_limit_bytes=_vmem_limit(nbytes)),
        name="layer_tail",
    )(x, y, w_mix_stack, g_mix.reshape(1, d), g_in.reshape(1, d), w_up_stack, w_dw, w_down_stack,
      g_out.reshape(1, d))


def _conv_mixer_kernel(x_ref, gin_ref, win_hbm, wdw_ref, o_ref, halo_ref, carry_ref, win_ref, stage, sem,
                       *, rows, sub_rows, d, chunk, layer):
    _load_weights_bf16([(win_hbm.at[layer], win_ref, stage, sem)])
    _zero_at_sequence_start(carry_ref)
    for t in range(rows // sub_rows):
        tile = slice(t * sub_rows, (t + 1) * sub_rows)
        xn = _rms(x_ref[0, tile, :], gin_ref[...]).astype(BF16)
        for c in range(d // chunk):
            cols = slice(c * chunk, (c + 1) * chunk)
            gate_b, gate_c, h = [
                jnp.dot(xn, win_ref[:, part * d + c * chunk:part * d + (c + 1) * chunk],
                        preferred_element_type=F32) for part in range(3)]
            conv = _causal_conv3(gate_c * h, halo_ref.at[c % 2], carry_ref.at[c], wdw_ref[:, cols],
                                 sub_rows)
            o_ref[0, tile, cols] = (gate_b * conv).astype(BF16)


def _conv_mixer(x, g_in, w_in_stack, layer, w_dw):
    b, s, d = x.shape
    in_shape = w_in_stack.shape[1:]
    chunk = MXU_DIM
    rows, sub_rows = BIG_ROW_TILE, ROW_TILE
    n_chunks = d // chunk
    nbytes = (math.prod(in_shape) * 2 + 8 * d * 4 + 2 * rows * d * 4 + 2 * rows * d * 2
              + 2 * (CONV_HALO + sub_rows) * chunk * 4 + n_chunks * CONV_HALO * chunk * 4
              + WEIGHT_SLOTS * WEIGHT_CHUNK_BYTES)
    return pl.pallas_call(
        functools.partial(_conv_mixer_kernel, rows=rows, sub_rows=sub_rows, d=d, chunk=chunk, layer=layer),
        out_shape=jax.ShapeDtypeStruct(x.shape, BF16),
        grid=(b, s // rows),
        in_specs=[
            _row_tiled(rows, d),
            _resident((1, d)),
            _IN_HBM,
            _resident(w_dw.shape),
        ],
        out_specs=_row_tiled(rows, d),
        scratch_shapes=[
            pltpu.VMEM((2, chunk // LANES, CONV_HALO + sub_rows, LANES), F32),
            pltpu.VMEM((n_chunks, CONV_HALO, chunk), F32),
            pltpu.VMEM(in_shape, BF16),
            *_weight_staging(in_shape[1], in_shape[0]),
        ],
        compiler_params=pltpu.CompilerParams(
            dimension_semantics=("arbitrary", "arbitrary"),
            vmem_limit_bytes=_vmem_limit(nbytes)),
        name="short_conv_mixer",
    )(x, g_in.reshape(1, d), w_in_stack, w_dw)


def _pool_mixer_kernel(x_ref, gin_ref, win_ref, wgrp_ref, scale_ref, o_ref, halo_ref, carry_ref,
                       *, rows, sub_rows, group_dim):
    _zero_at_sequence_start(carry_ref)
    slabs_per_group = group_dim // LANES
    for t in range(rows // sub_rows):
        tile = slice(t * sub_rows, (t + 1) * sub_rows)
        xn = _rms(x_ref[0, tile, :], gin_ref[...]).astype(BF16)
        u = jnp.dot(xn, win_ref[...], preferred_element_type=F32)
        pos = (pl.program_id(1) * rows + t * sub_rows + 1
               + lax.broadcasted_iota(jnp.int32, (sub_rows, LANES), 0))
        for g, w in enumerate(POOL_WINDOWS):
            inv_count = 1.0 / jnp.minimum(pos, w).astype(F32)
            pooled = []
            for s in range(g * slabs_per_group, (g + 1) * slabs_per_group):
                cols = slice(s * LANES, (s + 1) * LANES)
                us = u[:, cols]
                halo = halo_ref.at[t % 2, s]
                halo[0:POOL_HALO, :] = carry_ref[:, cols]
                halo[POOL_HALO:POOL_HALO + sub_rows, :] = us
                carry_ref[:, cols] = us[sub_rows - POOL_HALO:sub_rows, :]
                total = us
                for k in range(1, w):
                    total = total + halo[POOL_HALO - k:POOL_HALO - k + sub_rows, :]
                pooled.append((total * inv_count - us).astype(BF16))
            cols = slice(g * group_dim, (g + 1) * group_dim)
            yg = jnp.dot(jnp.concatenate(pooled, axis=1), wgrp_ref[g], preferred_element_type=F32)
            o_ref[0, tile, cols] = (yg * scale_ref[:, cols]).astype(BF16)


def _pool_mixer(x, g_in, w_in, w_grp, scale):
    b, s, d = x.shape
    n_groups, group_dim, _ = w_grp.shape
    assert n_groups == len(POOL_WINDOWS) and n_groups * group_dim == d
    rows, sub_rows = BIG_ROW_TILE, ROW_TILE
    nbytes = (w_in.size * 2 + w_grp.size * 2 + 2 * rows * d * 4 + 2 * rows * d * 2
              + 2 * (POOL_HALO + sub_rows) * d * 4 + POOL_HALO * d * 4)
    return pl.pallas_call(
        functools.partial(_pool_mixer_kernel, rows=rows, sub_rows=sub_rows, group_dim=group_dim),
        out_shape=jax.ShapeDtypeStruct(x.shape, BF16),
        grid=(b, s // rows),
        in_specs=[
            _row_tiled(rows, d),
            _resident((1, d)),
            _resident(w_in.shape),
            _resident(w_grp.shape),
            _resident((1, d)),
        ],
        out_specs=_row_tiled(rows, d),
        scratch_shapes=[
            pltpu.VMEM((2, d // LANES, POOL_HALO + sub_rows, LANES), F32),
            pltpu.VMEM((POOL_HALO, d), F32),
        ],
        compiler_params=pltpu.CompilerParams(
            dimension_semantics=("arbitrary", "arbitrary"),
            vmem_limit_bytes=_vmem_limit(nbytes)),
        name="pool_mixer",
    )(x, g_in.reshape(1, d), w_in.astype(BF16), w_grp.astype(BF16), scale.reshape(1, d))


def _qkv_kernel(x_ref, gin_ref, w_hbm, *refs, rows, sub_rows, d, n_pairs, layer):
    n_groups = len(DILATED_CFG)
    outs, (slab_ref, perm_ref, w_ref, stage, sem) = refs[:n_groups], refs[n_groups:]
    _load_weights_bf16([(w_hbm.at[layer], w_ref, stage, sem)])
    width = 3 * n_pairs * LANES
    n_slabs = d // LANES
    for t in range(rows // sub_rows):
        xn = _rms(x_ref[0, t * sub_rows:(t + 1) * sub_rows, :], gin_ref[...])
        slab = slab_ref.at[t % 2]
        for s in range(n_slabs):
            slab[s] = xn[:, s * LANES:(s + 1) * LANES]
        for g, (_, dilation) in enumerate(DILATED_CFG):
            sub = sub_rows // dilation
            if dilation == 1:
                xg = xn.astype(BF16)
            else:
                perm = perm_ref.at[t % 2, g % 2]
                for r in range(dilation):
                    for s in range(n_slabs):
                        perm[r * sub:(r + 1) * sub, s * LANES:(s + 1) * LANES] = (
                            slab[s, pl.ds(r, sub, stride=dilation), :].astype(BF16))
                xg = perm[...]
            y = jnp.dot(xg, w_ref[:, g * width:(g + 1) * width], preferred_element_type=F32)
            for i in range(3):
                for hp in range(n_pairs):
                    lo = (i * n_pairs + hp) * LANES
                    part = y[:, lo:lo + LANES]
                    if i == 0:
                        part = part * Q_SCALE
                    for r in range(dilation):
                        outs[g][0, i, hp, r, t * sub:(t + 1) * sub, :] = (
                            part[r * sub:(r + 1) * sub, :].astype(BF16))


def _qkv_proj(x, g_in, w_qkv_stack, layer, n_pairs):
    b, s, d = x.shape
    w_shape = w_qkv_stack.shape[1:]
    rows, sub_rows = BIG_ROW_TILE, ROW_TILE
    out_shapes, out_specs = [], []
    for _, dilation in DILATED_CFG:
        length = s // dilation
        out_shapes.append(jax.ShapeDtypeStruct((b, 3, n_pairs, dilation, length, LANES), BF16))
        out_specs.append(pl.BlockSpec((1, 3, n_pairs, dilation, rows // dilation, LANES),
                                      lambda bi, j: (bi, 0, 0, 0, j, 0)))
    nbytes = (math.prod(w_shape) * 2 + 2 * rows * d * 4 + 2 * rows * w_shape[1] * 2
              + 2 * sub_rows * d * 4 + 4 * sub_rows * d * 2 + WEIGHT_SLOTS * WEIGHT_CHUNK_BYTES)
    outs = pl.pallas_call(
        functools.partial(_qkv_kernel, rows=rows, sub_rows=sub_rows, d=d, n_pairs=n_pairs, layer=layer),
        out_shape=out_shapes,
        grid=(b, s // rows),
        in_specs=[_row_tiled(rows, d), _resident((1, d)), _IN_HBM],
        out_specs=out_specs,
        scratch_shapes=[
            pltpu.VMEM((2, d // LANES, sub_rows, LANES), F32),
            pltpu.VMEM((2, 2, sub_rows, d), BF16),
            pltpu.VMEM(w_shape, BF16),
            *_weight_staging(w_shape[1], w_shape[0]),
        ],
        compiler_params=pltpu.CompilerParams(
            dimension_semantics=("arbitrary", "arbitrary"),
            vmem_limit_bytes=_vmem_limit(nbytes)),
        name="qkv_proj",
    )(x, g_in.reshape(1, d), w_qkv_stack)
    return [o.reshape(b, 3, n_pairs, s, LANES) for o in outs]


def _attn_bias_table(n_pairs):
    n_heads = len(DILATED_CFG) * HEADS_PER_GROUP
    slopes = (2.0 ** (-8.0 * np.arange(1, n_heads + 1) / n_heads)).astype(np.float32)
    delta = ATTN_BLOCK + np.arange(ATTN_BLOCK)[:, None] - np.arange(2 * ATTN_BLOCK)[None, :]
    neg = np.float32(NEG_INF)
    tables = []
    for g, (window, dilation) in enumerate(DILATED_CFG):
        span = window // dilation
        valid = (delta >= 0) & (delta <= span)
        dist = (delta * dilation).astype(np.float32)
        with_prev = np.stack([np.where(valid, -slopes[g * HEADS_PER_GROUP + h] * dist * np.float32(LOG2_E), neg)
                              for h in range(HEADS_PER_GROUP)])
        with_prev = with_prev.reshape(n_pairs, 2, ATTN_BLOCK, 2 * ATTN_BLOCK)
        no_prev = with_prev.copy()
        no_prev[..., :ATTN_BLOCK] = neg
        row_zero = np.full_like(with_prev, neg)
        row_zero[..., :ATTN_BLOCK] = with_prev[..., ATTN_BLOCK:]
        tables += [with_prev, no_prev, row_zero]
    return jnp.asarray(np.stack(tables), F32)


WITH_PREV, NO_PREV, ROW_ZERO = 0, 1, 2
N_BIAS_KINDS = 3


def _attn_kernel(q0, k0, v0, q1, k1, v1, q2, k2, v2, bias_ref, o_ref, acc_ref, m_ref, l_ref, *, seq):
    blk = ATTN_BLOCK
    low_half = lax.broadcasted_iota(jnp.int32, (blk, LANES), 1) < HEAD_DIM

    def scores(kind, q_ref, k_ref, v_ref, q_start, k_start):
        q = q_ref[pl.ds(_aligned(q_start, blk), blk), :]
        k = k_ref[pl.ds(_aligned(k_start, blk), 2 * blk), :]
        v = v_ref[pl.ds(_aligned(k_start, blk), 2 * blk), :]
        v_ext = jnp.concatenate([v, jnp.ones((2 * blk, LANES), BF16)], axis=1)
        m_h, l_h, pv_h = [], [], []
        for a in range(2):
            qa = jnp.where(low_half if a == 0 else ~low_half, q, jnp.zeros_like(q))
            s = lax.dot_general(qa, k, (((1,), (1,)), ((), ())), preferred_element_type=F32)
            s = s + bias_ref[kind, a]
            m = jnp.max(s, axis=-1, keepdims=True)
            p = jnp.exp2(s - m).astype(BF16)
            pvl = jnp.dot(p, v_ext, preferred_element_type=F32)
            m_h.append(m)
            pv_h.append(pvl[:, :LANES])
            l_h.append(pvl[:, LANES:])
        return (jnp.where(low_half, m_h[0], m_h[1]), jnp.where(low_half, l_h[0], l_h[1]),
                jnp.where(low_half, pv_h[0], pv_h[1]))

    def merge(out_rows, first_group, m_b, l_b, pv_b):
        if first_group:
            m_ref[out_rows, :] = m_b
            l_ref[out_rows, :] = l_b
            acc_ref[out_rows, :] = pv_b
        else:
            m_old = m_ref[out_rows, :]
            m_new = jnp.maximum(m_old, m_b)
            a_old = jnp.exp2(m_old - m_new)
            a_blk = jnp.exp2(m_b - m_new)
            m_ref[out_rows, :] = m_new
            l_ref[out_rows, :] = l_ref[out_rows, :] * a_old + l_b * a_blk
            acc_ref[out_rows, :] = acc_ref[out_rows, :] * a_old + pv_b * a_blk

    groups = ((q0, k0, v0), (q1, k1, v1), (q2, k2, v2))
    total_blocks = seq // blk
    order = sorted(range(len(DILATED_CFG)), key=lambda g: -DILATED_CFG[g][1])
    for g in order:
        dilation, refs, first_group = DILATED_CFG[g][1], groups[g], g == order[0]
        n_blocks = total_blocks // dilation
        assert n_blocks & (n_blocks - 1) == 0 and total_blocks % ATTN_UNROLL == 0

        def step(it, carry, g=g, dilation=dilation, refs=refs, first_group=first_group, n_blocks=n_blocks):
            work = []
            for j in range(ATTN_UNROLL):
                f = it * ATTN_UNROLL + j
                r = lax.shift_right_logical(f, n_blocks.bit_length() - 1)
                n = lax.bitwise_and(f, n_blocks - 1)
                kind = jnp.where(n == 0, jnp.where(f == 0, ROW_ZERO, NO_PREV), WITH_PREV)
                start = n * (blk * dilation) + r
                rows = (pl.ds(_aligned(start, blk), blk) if dilation == 1
                        else pl.ds(start, blk, stride=dilation))
                work.append((scores(g * N_BIAS_KINDS + kind, *refs, f * blk, jnp.maximum(f - 1, 0) * blk),
                             rows))
            for vals, rows in work:
                merge(rows, first_group, *vals)
            return carry

        lax.fori_loop(0, total_blocks // ATTN_UNROLL, step, 0)

    o_ref[0] = (acc_ref[...] / l_ref[...]).astype(BF16)


def _attn_core(qkv_groups, bias):
    b, _, n_pairs, s, _ = qkv_groups[0].shape
    in_specs, args = [], []
    for qkv in qkv_groups:
        for i in range(3):
            in_specs.append(pl.BlockSpec((None, None, None, s, LANES),
                                         lambda bi, hp, i=i: (bi, i, hp, 0, 0)))
            args.append(qkv)
    n_kinds = bias.shape[0]
    in_specs.append(pl.BlockSpec((n_kinds, None, 2, ATTN_BLOCK, 2 * ATTN_BLOCK),
                                 lambda bi, hp: (0, hp, 0, 0, 0)))
    nbytes = (2 * 9 * s * LANES * 2 + 2 * n_kinds * 2 * ATTN_BLOCK * 2 * ATTN_BLOCK * 4
              + 2 * s * LANES * 2 + 3 * s * LANES * 4)
    return pl.pallas_call(
        functools.partial(_attn_kernel, seq=s),
        out_shape=jax.ShapeDtypeStruct((b, s, n_pairs * LANES), BF16),
        grid=(b, n_pairs),
        in_specs=in_specs,
        out_specs=pl.BlockSpec((1, s, LANES), lambda bi, hp: (bi, 0, hp)),
        scratch_shapes=[pltpu.VMEM((s, LANES), F32)] * 3,
        compiler_params=pltpu.CompilerParams(
            dimension_semantics=("arbitrary", "arbitrary"),
            vmem_limit_bytes=_vmem_limit(nbytes)),
        name="dilated_attn",
    )(*args, bias)


def _attn_mixer(x, g_in, w_qkv_stack, layer):
    n_pairs = w_qkv_stack.shape[2] // (len(DILATED_CFG) * 3 * LANES)
    assert n_pairs * 2 == HEADS_PER_GROUP and 2 * HEAD_DIM == LANES
    qkv_groups = _qkv_proj(x, g_in, w_qkv_stack, layer, n_pairs)
    return _attn_core(qkv_groups, _attn_bias_table(n_pairs))


def kernel(x, norm_g, attn_w_qkv, attn_w_o, conv_w_in, conv_w_dw, conv_w_out,
           pool_w_in, pool_w_grp, pool_scale, pool_w_out, ffn_w_up, ffn_w_dw, ffn_w_down):
    depth = norm_g.shape[0]
    ia = ib = ic = 0
    for i in range(depth):
        g = norm_g[i]
        kind = i % 3
        if kind == 0:
            y, w_mix, mix_layer = _attn_mixer(x, g[0], attn_w_qkv, ia), attn_w_o, ia
            ia += 1
        elif kind == 1:
            y, w_mix, mix_layer = _conv_mixer(x, g[0], conv_w_in, ib, conv_w_dw[ib]), conv_w_out, ib
            ib += 1
        else:
            y = _pool_mixer(x, g[0], pool_w_in[ic], pool_w_grp[ic], pool_scale[ic])
            w_mix, mix_layer = pool_w_out, ic
            ic += 1
        x = _layer_tail(x, y, w_mix, mix_layer, g[1], g[2], g[3], ffn_w_up, ffn_w_dw[i], ffn_w_down, i)
    return x
```

```python
import functools
import math

import numpy as np
import jax
import jax.numpy as jnp
from jax import lax
from jax.experimental import pallas as pl
from jax.experimental.pallas import tpu as pltpu

F32 = jnp.float32
BF16 = jnp.bfloat16

RMS_EPS = 1e-6
NEG_INF = -1e30

DILATED_CFG = ((128, 1), (512, 4), (2048, 16))
HEADS_PER_GROUP = 8
HEAD_DIM = 64
LOG2_E = 1.4426950408889634
Q_SCALE = HEAD_DIM ** -0.5 * LOG2_E
ATTN_BLOCK = 128
ATTN_UNROLL = 16
POOL_WINDOWS = (2, 4, 8, 16)

LANES = 128
BF16_SUBLANES = 16
MXU_DIM = 256
VMEM_CAP_BYTES = 60000 * 1024

ROW_TILE = 512
BIG_ROW_TILE = 1024
CONV_HALO = 8
POOL_HALO = 16
WEIGHT_CHUNK_BYTES = 3 << 18
WEIGHT_SLOTS = 4


def _vmem_limit(nbytes):
    return int(min(VMEM_CAP_BYTES, nbytes + (16 << 20)))


def _rms(x, g):
    ms = jnp.mean(x * x, axis=-1, keepdims=True)
    return x * lax.rsqrt(ms + RMS_EPS) * g


def _aligned(v, m):
    return v if isinstance(v, int) else pl.multiple_of(v, m)


def _resident(shape):
    nd = len(shape)
    return pl.BlockSpec(shape, lambda *_: (0,) * nd, pipeline_mode=pl.Buffered(1))


def _row_tiled(rows, width):
    return pl.BlockSpec((1, rows, width), lambda bi, i: (bi, i, 0))


_IN_HBM = pl.BlockSpec(memory_space=pl.ANY)


def _weight_staging(cols, *row_counts):
    common = functools.reduce(math.gcd, row_counts)
    rows = max(r for r in range(BF16_SUBLANES, common + 1, BF16_SUBLANES)
               if common % r == 0 and r * cols * 4 <= WEIGHT_CHUNK_BYTES)
    return [pltpu.VMEM((WEIGHT_SLOTS, rows, cols), F32), pltpu.SemaphoreType.DMA((WEIGHT_SLOTS,))]


def _load_weights_bf16(jobs):
    plan, used = [], {}
    for src, dst, stage, sems in jobs:
        rows = stage.shape[1]
        assert src.shape[0] % rows == 0 and src.shape[1] == stage.shape[2], (src.shape, stage.shape)
        for r0 in range(0, src.shape[0], rows):
            slot = used.get(id(stage), 0) % WEIGHT_SLOTS
            used[id(stage)] = used.get(id(stage), 0) + 1
            plan.append((src, dst, stage, sems, r0, rows, slot))

    def copy(i):
        src, _, stage, sems, r0, rows, slot = plan[i]
        return pltpu.make_async_copy(src.at[pl.ds(r0, rows), :], stage.at[slot], sems.at[slot])

    ahead = WEIGHT_SLOTS - 1

    @pl.when(jnp.logical_and(pl.program_id(0) == 0, pl.program_id(1) == 0))
    def _():
        for i in range(min(ahead, len(plan))):
            copy(i).start(priority=i % 2)
        for i, (_, dst, stage, _, r0, rows, slot) in enumerate(plan):
            if i + ahead < len(plan):
                copy(i + ahead).start(priority=(i + ahead) % 2)
            copy(i).wait()
            dst[r0:r0 + rows, :] = stage[slot].astype(BF16)


def _causal_conv3(cur, halo_ref, carry_ref, w, rows):
    out = []
    for s in range(cur.shape[1] // LANES):
        cols = slice(s * LANES, (s + 1) * LANES)
        piece = cur[:, cols]
        halo_ref[s, 0:CONV_HALO, :] = carry_ref[:, cols]
        halo_ref[s, CONV_HALO:CONV_HALO + rows, :] = piece
        carry_ref[:, cols] = piece[rows - CONV_HALO:rows, :]
        out.append(piece * w[2:3, cols]
                   + halo_ref[s, CONV_HALO - 1:CONV_HALO - 1 + rows, :] * w[1:2, cols]
                   + halo_ref[s, CONV_HALO - 2:CONV_HALO - 2 + rows, :] * w[0:1, cols])
    return jnp.concatenate(out, axis=1)


def _zero_at_sequence_start(carry_ref):
    @pl.when(pl.program_id(1) == 0)
    def _():
        carry_ref[...] = jnp.zeros(carry_ref.shape, carry_ref.dtype)


def _layer_tail_kernel(x_ref, y_ref, wmix_hbm, gmix_ref, gin_ref, wup_hbm, wdw_ref, wdown_hbm, gout_ref,
                       o_ref, halo_ref, carry_ref, act_ref, wmix_ref, wup_ref, wdown_ref,
                       stage_wide, sem_wide, stage, sem, *, rows, sub_rows, d_ff, chunk, mix_layer, ffn_layer):
    _load_weights_bf16([(wmix_hbm.at[mix_layer], wmix_ref, stage, sem),
                        (wup_hbm.at[ffn_layer], wup_ref, stage_wide, sem_wide),
                        (wdown_hbm.at[ffn_layer], wdown_ref, stage, sem)])
    _zero_at_sequence_start(carry_ref)
    for t in range(rows // sub_rows):
        tile = slice(t * sub_rows, (t + 1) * sub_rows)
        mixed = jnp.dot(y_ref[0, tile, :], wmix_ref[...], preferred_element_type=F32)
        x = x_ref[0, tile, :] + _rms(mixed, gmix_ref[...])
        xn = _rms(x, gin_ref[...]).astype(BF16)
        for c in range(d_ff // chunk):
            halves = []
            for part in range(2):
                cols = slice(part * d_ff + c * chunk, part * d_ff + (c + 1) * chunk)
                h = jnp.dot(xn, wup_ref[:, cols], preferred_element_type=F32)
                slot = 2 * c + part
                halves.append(_causal_conv3(h, halo_ref.at[slot % 4], carry_ref.at[slot],
                                            wdw_ref[:, cols], sub_rows))
            gate, lin = halves
            act_ref[tile, c * chunk:(c + 1) * chunk] = (gate * jax.nn.sigmoid(gate) * lin).astype(BF16)
        out = jnp.dot(act_ref[tile, :], wdown_ref[...], preferred_element_type=F32)
        o_ref[0, tile, :] = x + _rms(out, gout_ref[...])


def _layer_tail(x, y, w_mix_stack, mix_layer, g_mix, g_in, g_out, w_up_stack, w_dw, w_down_stack, ffn_layer):
    b, s, d = x.shape
    mix_shape, up_shape, down_shape = w_mix_stack.shape[1:], w_up_stack.shape[1:], w_down_stack.shape[1:]
    d_ff = down_shape[0]
    d_mix = y.shape[-1]
    chunk = MXU_DIM
    rows, sub_rows = BIG_ROW_TILE, ROW_TILE
    assert d_ff % chunk == 0 and s % rows == 0 and rows % sub_rows == 0 and mix_shape[0] == d_mix
    n_slots = 2 * (d_ff // chunk)
    staging = _weight_staging(up_shape[1], up_shape[0]) + _weight_staging(d, mix_shape[0], d_ff)
    nbytes = (2 * (math.prod(up_shape) + math.prod(down_shape) + math.prod(mix_shape)) + 8 * w_dw.shape[1] * 4
              + 4 * rows * d * 4 + 2 * rows * d_mix * 2
              + 4 * (CONV_HALO + sub_rows) * chunk * 4 + n_slots * CONV_HALO * chunk * 4
              + rows * d_ff * 2 + 2 * WEIGHT_SLOTS * WEIGHT_CHUNK_BYTES)
    return pl.pallas_call(
        functools.partial(_layer_tail_kernel, rows=rows, sub_rows=sub_rows, d_ff=d_ff, chunk=chunk,
                          mix_layer=mix_layer, ffn_layer=ffn_layer),
        out_shape=jax.ShapeDtypeStruct(x.shape, F32),
        grid=(b, s // rows),
        in_specs=[
            _row_tiled(rows, d),
            _row_tiled(rows, d_mix),
            _IN_HBM,
            _resident((1, d)),
            _resident((1, d)),
            _IN_HBM,
            _resident(w_dw.shape),
            _IN_HBM,
            _resident((1, d)),
        ],
        out_specs=_row_tiled(rows, d),
        scratch_shapes=[
            pltpu.VMEM((4, chunk // LANES, CONV_HALO + sub_rows, LANES), F32),
            pltpu.VMEM((n_slots, CONV_HALO, chunk), F32),
            pltpu.VMEM((rows, d_ff), BF16),
            pltpu.VMEM(mix_shape, BF16),
            pltpu.VMEM(up_shape, BF16),
            pltpu.VMEM(down_shape, BF16),
            *staging,
        ],
        compiler_params=pltpu.CompilerParams(
            dimension_semantics=("arbitrary", "arbitrary"),
            vmem_limit_bytes=_vmem_limit(nbytes)),
        name="layer_tail",
    )(x, y, w_mix_stack, g_mix.reshape(1, d), g_in.reshape(1, d), w_up_stack, w_dw, w_down_stack,
      g_out.reshape(1, d))


def _conv_mixer_kernel(x_ref, gin_ref, win_hbm, wdw_ref, o_ref, halo_ref, carry_ref, win_ref, stage, sem,
                       *, rows, sub_rows, d, chunk, layer):
    _load_weights_bf16([(win_hbm.at[layer], win_ref, stage, sem)])
    _zero_at_sequence_start(carry_ref)
    for t in range(rows // sub_rows):
        tile = slice(t * sub_rows, (t + 1) * sub_rows)
        xn = _rms(x_ref[0, tile, :], gin_ref[...]).astype(BF16)
        for c in range(d // chunk):
            cols = slice(c * chunk, (c + 1) * chunk)
            gate_b, gate_c, h = [
                jnp.dot(xn, win_ref[:, part * d + c * chunk:part * d + (c + 1) * chunk],
                        preferred_element_type=F32) for part in range(3)]
            conv = _causal_conv3(gate_c * h, halo_ref.at[c % 2], carry_ref.at[c], wdw_ref[:, cols],
                                 sub_rows)
            o_ref[0, tile, cols] = (gate_b * conv).astype(BF16)


def _conv_mixer(x, g_in, w_in_stack, layer, w_dw):
    b, s, d = x.shape
    in_shape = w_in_stack.shape[1:]
    chunk = MXU_DIM
    rows, sub_rows = BIG_ROW_TILE, ROW_TILE
    n_chunks = d // chunk
    nbytes = (math.prod(in_shape) * 2 + 8 * d * 4 + 2 * rows * d * 4 + 2 * rows * d * 2
              + 2 * (CONV_HALO + sub_rows) * chunk * 4 + n_chunks * CONV_HALO * chunk * 4
              + WEIGHT_SLOTS * WEIGHT_CHUNK_BYTES)
    return pl.pallas_call(
        functools.partial(_conv_mixer_kernel, rows=rows, sub_rows=sub_rows, d=d, chunk=chunk, layer=layer),
        out_shape=jax.ShapeDtypeStruct(x.shape, BF16),
        grid=(b, s // rows),
        in_specs=[
            _row_tiled(rows, d),
            _resident((1, d)),
            _IN_HBM,
            _resident(w_dw.shape),
        ],
        out_specs=_row_tiled(rows, d),
        scratch_shapes=[
            pltpu.VMEM((2, chunk // LANES, CONV_HALO + sub_rows, LANES), F32),
            pltpu.VMEM((n_chunks, CONV_HALO, chunk), F32),
            pltpu.VMEM(in_shape, BF16),
            *_weight_staging(in_shape[1], in_shape[0]),
        ],
        compiler_params=pltpu.CompilerParams(
            dimension_semantics=("arbitrary", "arbitrary"),
            vmem_limit_bytes=_vmem_limit(nbytes)),
        name="short_conv_mixer",
    )(x, g_in.reshape(1, d), w_in_stack, w_dw)


def _pool_mixer_kernel(x_ref, gin_ref, win_ref, wgrp_ref, scale_ref, o_ref, halo_ref, carry_ref,
                       *, rows, sub_rows, group_dim):
    _zero_at_sequence_start(carry_ref)
    slabs_per_group = group_dim // LANES
    for t in range(rows // sub_rows):
        tile = slice(t * sub_rows, (t + 1) * sub_rows)
        xn = _rms(x_ref[0, tile, :], gin_ref[...]).astype(BF16)
        u = jnp.dot(xn, win_ref[...], preferred_element_type=F32)
        pos = (pl.program_id(1) * rows + t * sub_rows + 1
               + lax.broadcasted_iota(jnp.int32, (sub_rows, LANES), 0))
        for g, w in enumerate(POOL_WINDOWS):
            inv_count = 1.0 / jnp.minimum(pos, w).astype(F32)
            pooled = []
            for s in range(g * slabs_per_group, (g + 1) * slabs_per_group):
                cols = slice(s * LANES, (s + 1) * LANES)
                us = u[:, cols]
                halo = halo_ref.at[t % 2, s]
                halo[0:POOL_HALO, :] = carry_ref[:, cols]
                halo[POOL_HALO:POOL_HALO + sub_rows, :] = us
                carry_ref[:, cols] = us[sub_rows - POOL_HALO:sub_rows, :]
                total = us
                for k in range(1, w):
                    total = total + halo[POOL_HALO - k:POOL_HALO - k + sub_rows, :]
                pooled.append((total * inv_count - us).astype(BF16))
            cols = slice(g * group_dim, (g + 1) * group_dim)
            yg = jnp.dot(jnp.concatenate(pooled, axis=1), wgrp_ref[g], preferred_element_type=F32)
            o_ref[0, tile, cols] = (yg * scale_ref[:, cols]).astype(BF16)


def _pool_mixer(x, g_in, w_in, w_grp, scale):
    b, s, d = x.shape
    n_groups, group_dim, _ = w_grp.shape
    assert n_groups == len(POOL_WINDOWS) and n_groups * group_dim == d
    rows, sub_rows = BIG_ROW_TILE, ROW_TILE
    nbytes = (w_in.size * 2 + w_grp.size * 2 + 2 * rows * d * 4 + 2 * rows * d * 2
              + 2 * (POOL_HALO + sub_rows) * d * 4 + POOL_HALO * d * 4)
    return pl.pallas_call(
        functools.partial(_pool_mixer_kernel, rows=rows, sub_rows=sub_rows, group_dim=group_dim),
        out_shape=jax.ShapeDtypeStruct(x.shape, BF16),
        grid=(b, s // rows),
        in_specs=[
            _row_tiled(rows, d),
            _resident((1, d)),
            _resident(w_in.shape),
            _resident(w_grp.shape),
            _resident((1, d)),
        ],
        out_specs=_row_tiled(rows, d),
        scratch_shapes=[
            pltpu.VMEM((2, d // LANES, POOL_HALO + sub_rows, LANES), F32),
            pltpu.VMEM((POOL_HALO, d), F32),
        ],
        compiler_params=pltpu.CompilerParams(
            dimension_semantics=("arbitrary", "arbitrary"),
            vmem_limit_bytes=_vmem_limit(nbytes)),
        name="pool_mixer",
    )(x, g_in.reshape(1, d), w_in.astype(BF16), w_grp.astype(BF16), scale.reshape(1, d))


def _qkv_kernel(x_ref, gin_ref, w_hbm, *refs, rows, sub_rows, d, n_pairs, layer):
    n_groups = len(DILATED_CFG)
    outs, (slab_ref, perm_ref, w_ref, stage, sem) = refs[:n_groups], refs[n_groups:]
    _load_weights_bf16([(w_hbm.at[layer], w_ref, stage, sem)])
    width = 3 * n_pairs * LANES
    n_slabs = d // LANES
    for t in range(rows // sub_rows):
        xn = _rms(x_ref[0, t * sub_rows:(t + 1) * sub_rows, :], gin_ref[...])
        slab = slab_ref.at[t % 2]
        for s in range(n_slabs):
            slab[s] = xn[:, s * LANES:(s + 1) * LANES]
        for g, (_, dilation) in enumerate(DILATED_CFG):
            sub = sub_rows // dilation
            if dilation == 1:
                xg = xn.astype(BF16)
            else:
                perm = perm_ref.at[t % 2, g % 2]
                for r in range(dilation):
                    for s in range(n_slabs):
                        perm[r * sub:(r + 1) * sub, s * LANES:(s + 1) * LANES] = (
                            slab[s, pl.ds(r, sub, stride=dilation), :].astype(BF16))
                xg = perm[...]
            y = jnp.dot(xg, w_ref[:, g * width:(g + 1) * width], preferred_element_type=F32)
            for i in range(3):
                for hp in range(n_pairs):
                    lo = (i * n_pairs + hp) * LANES
                    part = y[:, lo:lo + LANES]
                    if i == 0:
                        part = part * Q_SCALE
                    for r in range(dilation):
                        outs[g][0, i, hp, r, t * sub:(t + 1) * sub, :] = (
                            part[r * sub:(r + 1) * sub, :].astype(BF16))


def _qkv_proj(x, g_in, w_qkv_stack, layer, n_pairs):
    b, s, d = x.shape
    w_shape = w_qkv_stack.shape[1:]
    rows, sub_rows = BIG_ROW_TILE, ROW_TILE
    out_shapes, out_specs = [], []
    for _, dilation in DILATED_CFG:
        length = s // dilation
        out_shapes.append(jax.ShapeDtypeStruct((b, 3, n_pairs, dilation, length, LANES), BF16))
        out_specs.append(pl.BlockSpec((1, 3, n_pairs, dilation, rows // dilation, LANES),
                                      lambda bi, j: (bi, 0, 0, 0, j, 0)))
    nbytes = (math.prod(w_shape) * 2 + 2 * rows * d * 4 + 2 * rows * w_shape[1] * 2
              + 2 * sub_rows * d * 4 + 4 * sub_rows * d * 2 + WEIGHT_SLOTS * WEIGHT_CHUNK_BYTES)
    outs = pl.pallas_call(
        functools.partial(_qkv_kernel, rows=rows, sub_rows=sub_rows, d=d, n_pairs=n_pairs, layer=layer),
        out_shape=out_shapes,
        grid=(b, s // rows),
        in_specs=[_row_tiled(rows, d), _resident((1, d)), _IN_HBM],
        out_specs=out_specs,
        scratch_shapes=[
            pltpu.VMEM((2, d // LANES, sub_rows, LANES), F32),
            pltpu.VMEM((2, 2, sub_rows, d), BF16),
            pltpu.VMEM(w_shape, BF16),
            *_weight_staging(w_shape[1], w_shape[0]),
        ],
        compiler_params=pltpu.CompilerParams(
            dimension_semantics=("arbitrary", "arbitrary"),
            vmem_limit_bytes=_vmem_limit(nbytes)),
        name="qkv_proj",
    )(x, g_in.reshape(1, d), w_qkv_stack)
    return [o.reshape(b, 3, n_pairs, s, LANES) for o in outs]


def _attn_bias_table(n_pairs):
    n_heads = len(DILATED_CFG) * HEADS_PER_GROUP
    slopes = (2.0 ** (-8.0 * np.arange(1, n_heads + 1) / n_heads)).astype(np.float32)
    delta = ATTN_BLOCK + np.arange(ATTN_BLOCK)[:, None] - np.arange(2 * ATTN_BLOCK)[None, :]
    neg = np.float32(NEG_INF)
    tables = []
    for g, (window, dilation) in enumerate(DILATED_CFG):
        span = window // dilation
        valid = (delta >= 0) & (delta <= span)
        dist = (delta * dilation).astype(np.float32)
        with_prev = np.stack([np.where(valid, -slopes[g * HEADS_PER_GROUP + h] * dist * np.float32(LOG2_E), neg)
                              for h in range(HEADS_PER_GROUP)])
        with_prev = with_prev.reshape(n_pairs, 2, ATTN_BLOCK, 2 * ATTN_BLOCK)
        no_prev = with_prev.copy()
        no_prev[..., :ATTN_BLOCK] = neg
        row_zero = np.full_like(with_prev, neg)
        row_zero[..., :ATTN_BLOCK] = with_prev[..., ATTN_BLOCK:]
        tables += [with_prev, no_prev, row_zero]
    return jnp.asarray(np.stack(tables), F32)


WITH_PREV, NO_PREV, ROW_ZERO = 0, 1, 2
N_BIAS_KINDS = 3


def _attn_kernel(q0, k0, v0, q1, k1, v1, q2, k2, v2, bias_ref, o_ref, acc_ref, m_ref, l_ref, *, seq):
    blk = ATTN_BLOCK
    low_half = lax.broadcasted_iota(jnp.int32, (blk, LANES), 1) < HEAD_DIM

    def scores(kind, q_ref, k_ref, v_ref, q_start, k_start):
        q = q_ref[pl.ds(_aligned(q_start, blk), blk), :]
        k = k_ref[pl.ds(_aligned(k_start, blk), 2 * blk), :]
        v = v_ref[pl.ds(_aligned(k_start, blk), 2 * blk), :]
        v_ext = jnp.concatenate([v, jnp.ones((2 * blk, LANES), BF16)], axis=1)
        m_h, l_h, pv_h = [], [], []
        for a in range(2):
            qa = jnp.where(low_half if a == 0 else ~low_half, q, jnp.zeros_like(q))
            s = lax.dot_general(qa, k, (((1,), (1,)), ((), ())), preferred_element_type=F32)
            s = s + bias_ref[kind, a]
            m = jnp.max(s, axis=-1, keepdims=True)
            p = jnp.exp2(s - m).astype(BF16)
            pvl = jnp.dot(p, v_ext, preferred_element_type=F32)
            m_h.append(m)
            pv_h.append(pvl[:, :LANES])
            l_h.append(pvl[:, LANES:])
        return (jnp.where(low_half, m_h[0], m_h[1]), jnp.where(low_half, l_h[0], l_h[1]),
                jnp.where(low_half, pv_h[0], pv_h[1]))

    def merge(out_rows, first_group, m_b, l_b, pv_b):
        if first_group:
            m_ref[out_rows, :] = m_b
            l_ref[out_rows, :] = l_b
            acc_ref[out_rows, :] = pv_b
        else:
            m_old = m_ref[out_rows, :]
            m_new = jnp.maximum(m_old, m_b)
            a_old = jnp.exp2(m_old - m_new)
            a_blk = jnp.exp2(m_b - m_new)
            m_ref[out_rows, :] = m_new
            l_ref[out_rows, :] = l_ref[out_rows, :] * a_old + l_b * a_blk
            acc_ref[out_rows, :] = acc_ref[out_rows, :] * a_old + pv_b * a_blk

    groups = ((q0, k0, v0), (q1, k1, v1), (q2, k2, v2))
    total_blocks = seq // blk
    order = sorted(range(len(DILATED_CFG)), key=lambda g: -DILATED_CFG[g][1])
    for g in order:
        dilation, refs, first_group = DILATED_CFG[g][1], groups[g], g == order[0]
        n_blocks = total_blocks // dilation
        assert n_blocks & (n_blocks - 1) == 0 and total_blocks % ATTN_UNROLL == 0

        def step(it, carry, g=g, dilation=dilation, refs=refs, first_group=first_group, n_blocks=n_blocks):
            work = []
            for j in range(ATTN_UNROLL):
                f = it * ATTN_UNROLL + j
                r = lax.shift_right_logical(f, n_blocks.bit_length() - 1)
                n = lax.bitwise_and(f, n_blocks - 1)
                kind = jnp.where(n == 0, jnp.where(f == 0, ROW_ZERO, NO_PREV), WITH_PREV)
                start = n * (blk * dilation) + r
                rows = (pl.ds(_aligned(start, blk), blk) if dilation == 1
                        else pl.ds(start, blk, stride=dilation))
                work.append((scores(g * N_BIAS_KINDS + kind, *refs, f * blk, jnp.maximum(f - 1, 0) * blk),
                             rows))
            for vals, rows in work:
                merge(rows, first_group, *vals)
            return carry

        lax.fori_loop(0, total_blocks // ATTN_UNROLL, step, 0)

    o_ref[0] = (acc_ref[...] / l_ref[...]).astype(BF16)


def _attn_core(qkv_groups, bias):
    b, _, n_pairs, s, _ = qkv_groups[0].shape
    in_specs, args = [], []
    for qkv in qkv_groups:
        for i in range(3):
            in_specs.append(pl.BlockSpec((None, None, None, s, LANES),
                                         lambda bi, hp, i=i: (bi, i, hp, 0, 0)))
            args.append(qkv)
    n_kinds = bias.shape[0]
    in_specs.append(pl.BlockSpec((n_kinds, None, 2, ATTN_BLOCK, 2 * ATTN_BLOCK),
                                 lambda bi, hp: (0, hp, 0, 0, 0)))
    nbytes = (2 * 9 * s * LANES * 2 + 2 * n_kinds * 2 * ATTN_BLOCK * 2 * ATTN_BLOCK * 4
              + 2 * s * LANES * 2 + 3 * s * LANES * 4)
    return pl.pallas_call(
        functools.partial(_attn_kernel, seq=s),
        out_shape=jax.ShapeDtypeStruct((b, s, n_pairs * LANES), BF16),
        grid=(b, n_pairs),
        in_specs=in_specs,
        out_specs=pl.BlockSpec((1, s, LANES), lambda bi, hp: (bi, 0, hp)),
        scratch_shapes=[pltpu.VMEM((s, LANES), F32)] * 3,
        compiler_params=pltpu.CompilerParams(
            dimension_semantics=("arbitrary", "arbitrary"),
            vmem_limit_bytes=_vmem_limit(nbytes)),
        name="dilated_attn",
    )(*args, bias)


def _attn_mixer(x, g_in, w_qkv_stack, layer):
    n_pairs = w_qkv_stack.shape[2] // (len(DILATED_CFG) * 3 * LANES)
    assert n_pairs * 2 == HEADS_PER_GROUP and 2 * HEAD_DIM == LANES
    qkv_groups = _qkv_proj(x, g_in, w_qkv_stack, layer, n_pairs)
    return _attn_core(qkv_groups, _attn_bias_table(n_pairs))


def kernel(x, norm_g, attn_w_qkv, attn_w_o, conv_w_in, conv_w_dw, conv_w_out,
           pool_w_in, pool_w_grp, pool_scale, pool_w_out, ffn_w_up, ffn_w_dw, ffn_w_down):
    depth = norm_g.shape[0]
    ia = ib = ic = 0
    for i in range(depth):
        g = norm_g[i]
        kind = i % 3
        if kind == 0:
            y, w_mix, mix_layer = _attn_mixer(x, g[0], attn_w_qkv, ia), attn_w_o, ia
            ia += 1
        elif kind == 1:
            y, w_mix, mix_layer = _conv_mixer(x, g[0], conv_w_in, ib, conv_w_dw[ib]), conv_w_out, ib
            ib += 1
        else:
            y = _pool_mixer(x, g[0], pool_w_in[ic], pool_w_grp[ic], pool_scale[ic])
            w_mix, mix_layer = pool_w_out, ic
            ic += 1
        x = _layer_tail(x, y, w_mix, mix_layer, g[1], g[2], g[3], ffn_w_up, ffn_w_dw[i], ffn_w_down, i)
    return x
```

```python
import functools
import math

import numpy as np
import jax
import jax.numpy as jnp
from jax import lax
from jax.experimental import pallas as pl
from jax.experimental.pallas import tpu as pltpu

F32 = jnp.float32
BF16 = jnp.bfloat16

RMS_EPS = 1e-6
NEG_INF = -1e30

DILATED_CFG = ((128, 1), (512, 4), (2048, 16))
HEADS_PER_GROUP = 8
HEAD_DIM = 64
LOG2_E = 1.4426950408889634
Q_SCALE = HEAD_DIM ** -0.5 * LOG2_E
ATTN_BLOCK = 128
ATTN_UNROLL = 16
POOL_WINDOWS = (2, 4, 8, 16)

LANES = 128
BF16_SUBLANES = 16
MXU_DIM = 256
VMEM_CAP_BYTES = 60000 * 1024

ROW_TILE = 512
BIG_ROW_TILE = 1024
CONV_HALO = 8
POOL_HALO = 16
WEIGHT_CHUNK_BYTES = 1 << 19
WEIGHT_SLOTS = 6


def _vmem_limit(nbytes):
    return int(min(VMEM_CAP_BYTES, nbytes + (16 << 20)))


def _rms(x, g):
    ms = jnp.mean(x * x, axis=-1, keepdims=True)
    return x * lax.rsqrt(ms + RMS_EPS) * g


def _aligned(v, m):
    return v if isinstance(v, int) else pl.multiple_of(v, m)


def _resident(shape):
    nd = len(shape)
    return pl.BlockSpec(shape, lambda *_: (0,) * nd, pipeline_mode=pl.Buffered(1))


def _row_tiled(rows, width):
    return pl.BlockSpec((1, rows, width), lambda bi, i: (bi, i, 0))


_IN_HBM = pl.BlockSpec(memory_space=pl.ANY)


def _weight_staging(cols, *row_counts):
    common = functools.reduce(math.gcd, row_counts)
    rows = max(r for r in range(BF16_SUBLANES, common + 1, BF16_SUBLANES)
               if common % r == 0 and r * cols * 4 <= WEIGHT_CHUNK_BYTES)
    return [pltpu.VMEM((WEIGHT_SLOTS, rows, cols), F32), pltpu.SemaphoreType.DMA((WEIGHT_SLOTS,))]


def _load_weights_bf16(jobs):
    plan, used = [], {}
    for src, dst, stage, sems in jobs:
        rows = stage.shape[1]
        assert src.shape[0] % rows == 0 and src.shape[1] == stage.shape[2], (src.shape, stage.shape)
        for r0 in range(0, src.shape[0], rows):
            slot = used.get(id(stage), 0) % WEIGHT_SLOTS
            used[id(stage)] = used.get(id(stage), 0) + 1
            plan.append((src, dst, stage, sems, r0, rows, slot))

    def copy(i):
        src, _, stage, sems, r0, rows, slot = plan[i]
        return pltpu.make_async_copy(src.at[pl.ds(r0, rows), :], stage.at[slot], sems.at[slot])

    ahead = WEIGHT_SLOTS - 1

    @pl.when(jnp.logical_and(pl.program_id(0) == 0, pl.program_id(1) == 0))
    def _():
        for i in range(min(ahead, len(plan))):
            copy(i).start()
        for i, (_, dst, stage, _, r0, rows, slot) in enumerate(plan):
            if i + ahead < len(plan):
                copy(i + ahead).start()
            copy(i).wait()
            dst[r0:r0 + rows, :] = stage[slot].astype(BF16)


def _causal_conv3(cur, halo_ref, carry_ref, w, rows):
    out = []
    for s in range(cur.shape[1] // LANES):
        cols = slice(s * LANES, (s + 1) * LANES)
        piece = cur[:, cols]
        halo_ref[s, 0:CONV_HALO, :] = carry_ref[:, cols]
        halo_ref[s, CONV_HALO:CONV_HALO + rows, :] = piece
        carry_ref[:, cols] = piece[rows - CONV_HALO:rows, :]
        out.append(piece * w[2:3, cols]
                   + halo_ref[s, CONV_HALO - 1:CONV_HALO - 1 + rows, :] * w[1:2, cols]
                   + halo_ref[s, CONV_HALO - 2:CONV_HALO - 2 + rows, :] * w[0:1, cols])
    return jnp.concatenate(out, axis=1)


def _zero_at_sequence_start(carry_ref):
    @pl.when(pl.program_id(1) == 0)
    def _():
        carry_ref[...] = jnp.zeros(carry_ref.shape, carry_ref.dtype)


def _layer_tail_kernel(x_ref, y_ref, wmix_hbm, gmix_ref, gin_ref, wup_hbm, wdw_ref, wdown_hbm, gout_ref,
                       o_ref, halo_ref, carry_ref, act_ref, wmix_ref, wup_ref, wdown_ref,
                       stage_wide, sem_wide, stage, sem, *, rows, sub_rows, d_ff, chunk, mix_layer, ffn_layer):
    _load_weights_bf16([(wmix_hbm.at[mix_layer], wmix_ref, stage, sem),
                        (wup_hbm.at[ffn_layer], wup_ref, stage_wide, sem_wide),
                        (wdown_hbm.at[ffn_layer], wdown_ref, stage, sem)])
    _zero_at_sequence_start(carry_ref)
    for t in range(rows // sub_rows):
        tile = slice(t * sub_rows, (t + 1) * sub_rows)
        mixed = jnp.dot(y_ref[0, tile, :], wmix_ref[...], preferred_element_type=F32)
        x = x_ref[0, tile, :] + _rms(mixed, gmix_ref[...])
        xn = _rms(x, gin_ref[...]).astype(BF16)
        for c in range(d_ff // chunk):
            halves = []
            for part in range(2):
                cols = slice(part * d_ff + c * chunk, part * d_ff + (c + 1) * chunk)
                h = jnp.dot(xn, wup_ref[:, cols], preferred_element_type=F32)
                slot = 2 * c + part
                halves.append(_causal_conv3(h, halo_ref.at[slot % 4], carry_ref.at[slot],
                                            wdw_ref[:, cols], sub_rows))
            gate, lin = halves
            act_ref[tile, c * chunk:(c + 1) * chunk] = (gate * jax.nn.sigmoid(gate) * lin).astype(BF16)
        out = jnp.dot(act_ref[tile, :], wdown_ref[...], preferred_element_type=F32)
        o_ref[0, tile, :] = x + _rms(out, gout_ref[...])


def _layer_tail(x, y, w_mix_stack, mix_layer, g_mix, g_in, g_out, w_up_stack, w_dw, w_down_stack, ffn_layer):
    b, s, d = x.shape
    mix_shape, up_shape, down_shape = w_mix_stack.shape[1:], w_up_stack.shape[1:], w_down_stack.shape[1:]
    d_ff = down_shape[0]
    d_mix = y.shape[-1]
    chunk = MXU_DIM
    rows, sub_rows = BIG_ROW_TILE, ROW_TILE
    assert d_ff % chunk == 0 and s % rows == 0 and rows % sub_rows == 0 and mix_shape[0] == d_mix
    n_slots = 2 * (d_ff // chunk)
    staging = _weight_staging(up_shape[1], up_shape[0]) + _weight_staging(d, mix_shape[0], d_ff)
    nbytes = (2 * (math.prod(up_shape) + math.prod(down_shape) + math.prod(mix_shape)) + 8 * w_dw.shape[1] * 4
              + 4 * rows * d * 4 + 2 * rows * d_mix * 2
              + 4 * (CONV_HALO + sub_rows) * chunk * 4 + n_slots * CONV_HALO * chunk * 4
              + rows * d_ff * 2 + 2 * WEIGHT_SLOTS * WEIGHT_CHUNK_BYTES)
    return pl.pallas_call(
        functools.partial(_layer_tail_kernel, rows=rows, sub_rows=sub_rows, d_ff=d_ff, chunk=chunk,
                          mix_layer=mix_layer, ffn_layer=ffn_layer),
        out_shape=jax.ShapeDtypeStruct(x.shape, F32),
        grid=(b, s // rows),
        in_specs=[
            _row_tiled(rows, d),
            _row_tiled(rows, d_mix),
            _IN_HBM,
            _resident((1, d)),
            _resident((1, d)),
            _IN_HBM,
            _resident(w_dw.shape),
            _IN_HBM,
            _resident((1, d)),
        ],
        out_specs=_row_tiled(rows, d),
        scratch_shapes=[
            pltpu.VMEM((4, chunk // LANES, CONV_HALO + sub_rows, LANES), F32),
            pltpu.VMEM((n_slots, CONV_HALO, chunk), F32),
            pltpu.VMEM((rows, d_ff), BF16),
            pltpu.VMEM(mix_shape, BF16),
            pltpu.VMEM(up_shape, BF16),
            pltpu.VMEM(down_shape, BF16),
            *staging,
        ],
        compiler_params=pltpu.CompilerParams(
            dimension_semantics=("arbitrary", "arbitrary"),
            vmem_limit_bytes=_vmem_limit(nbytes)),
        name="layer_tail",
    )(x, y, w_mix_stack, g_mix.reshape(1, d), g_in.reshape(1, d), w_up_stack, w_dw, w_down_stack,
      g_out.reshape(1, d))


def _conv_mixer_kernel(x_ref, gin_ref, win_hbm, wdw_ref, o_ref, halo_ref, carry_ref, win_ref, stage, sem,
                       *, rows, sub_rows, d, chunk, layer):
    _load_weights_bf16([(win_hbm.at[layer], win_ref, stage, sem)])
    _zero_at_sequence_start(carry_ref)
    for t in range(rows // sub_rows):
        tile = slice(t * sub_rows, (t + 1) * sub_rows)
        xn = _rms(x_ref[0, tile, :], gin_ref[...]).astype(BF16)
        for c in range(d // chunk):
            cols = slice(c * chunk, (c + 1) * chunk)
            gate_b, gate_c, h = [
                jnp.dot(xn, win_ref[:, part * d + c * chunk:part * d + (c + 1) * chunk],
                        preferred_element_type=F32) for part in range(3)]
            conv = _causal_conv3(gate_c * h, halo_ref.at[c % 2], carry_ref.at[c], wdw_ref[:, cols],
                                 sub_rows)
            o_ref[0, tile, cols] = (gate_b * conv).astype(BF16)


def _conv_mixer(x, g_in, w_in_stack, layer, w_dw):
    b, s, d = x.shape
    in_shape = w_in_stack.shape[1:]
    chunk = MXU_DIM
    rows, sub_rows = BIG_ROW_TILE, ROW_TILE
    n_chunks = d // chunk
    nbytes = (math.prod(in_shape) * 2 + 8 * d * 4 + 2 * rows * d * 4 + 2 * rows * d * 2
              + 2 * (CONV_HALO + sub_rows) * chunk * 4 + n_chunks * CONV_HALO * chunk * 4
              + WEIGHT_SLOTS * WEIGHT_CHUNK_BYTES)
    return pl.pallas_call(
        functools.partial(_conv_mixer_kernel, rows=rows, sub_rows=sub_rows, d=d, chunk=chunk, layer=layer),
        out_shape=jax.ShapeDtypeStruct(x.shape, BF16),
        grid=(b, s // rows),
        in_specs=[
            _row_tiled(rows, d),
            _resident((1, d)),
            _IN_HBM,
            _resident(w_dw.shape),
        ],
        out_specs=_row_tiled(rows, d),
        scratch_shapes=[
            pltpu.VMEM((2, chunk // LANES, CONV_HALO + sub_rows, LANES), F32),
            pltpu.VMEM((n_chunks, CONV_HALO, chunk), F32),
            pltpu.VMEM(in_shape, BF16),
            *_weight_staging(in_shape[1], in_shape[0]),
        ],
        compiler_params=pltpu.CompilerParams(
            dimension_semantics=("arbitrary", "arbitrary"),
            vmem_limit_bytes=_vmem_limit(nbytes)),
        name="short_conv_mixer",
    )(x, g_in.reshape(1, d), w_in_stack, w_dw)


def _pool_mixer_kernel(x_ref, gin_ref, win_ref, wgrp_ref, scale_ref, o_ref, halo_ref, carry_ref,
                       *, rows, sub_rows, group_dim):
    _zero_at_sequence_start(carry_ref)
    slabs_per_group = group_dim // LANES
    for t in range(rows // sub_rows):
        tile = slice(t * sub_rows, (t + 1) * sub_rows)
        xn = _rms(x_ref[0, tile, :], gin_ref[...]).astype(BF16)
        u = jnp.dot(xn, win_ref[...], preferred_element_type=F32)
        pos = (pl.program_id(1) * rows + t * sub_rows + 1
               + lax.broadcasted_iota(jnp.int32, (sub_rows, LANES), 0))
        for g, w in enumerate(POOL_WINDOWS):
            inv_count = 1.0 / jnp.minimum(pos, w).astype(F32)
            pooled = []
            for s in range(g * slabs_per_group, (g + 1) * slabs_per_group):
                cols = slice(s * LANES, (s + 1) * LANES)
                us = u[:, cols]
                halo = halo_ref.at[t % 2, s]
                halo[0:POOL_HALO, :] = carry_ref[:, cols]
                halo[POOL_HALO:POOL_HALO + sub_rows, :] = us
                carry_ref[:, cols] = us[sub_rows - POOL_HALO:sub_rows, :]
                total = us
                for k in range(1, w):
                    total = total + halo[POOL_HALO - k:POOL_HALO - k + sub_rows, :]
                pooled.append((total * inv_count - us).astype(BF16))
            cols = slice(g * group_dim, (g + 1) * group_dim)
            yg = jnp.dot(jnp.concatenate(pooled, axis=1), wgrp_ref[g], preferred_element_type=F32)
            o_ref[0, tile, cols] = (yg * scale_ref[:, cols]).astype(BF16)


def _pool_mixer(x, g_in, w_in, w_grp, scale):
    b, s, d = x.shape
    n_groups, group_dim, _ = w_grp.shape
    assert n_groups == len(POOL_WINDOWS) and n_groups * group_dim == d
    rows, sub_rows = BIG_ROW_TILE, ROW_TILE
    nbytes = (w_in.size * 2 + w_grp.size * 2 + 2 * rows * d * 4 + 2 * rows * d * 2
              + 2 * (POOL_HALO + sub_rows) * d * 4 + POOL_HALO * d * 4)
    return pl.pallas_call(
        functools.partial(_pool_mixer_kernel, rows=rows, sub_rows=sub_rows, group_dim=group_dim),
        out_shape=jax.ShapeDtypeStruct(x.shape, BF16),
        grid=(b, s // rows),
        in_specs=[
            _row_tiled(rows, d),
            _resident((1, d)),
            _resident(w_in.shape),
            _resident(w_grp.shape),
            _resident((1, d)),
        ],
        out_specs=_row_tiled(rows, d),
        scratch_shapes=[
            pltpu.VMEM((2, d // LANES, POOL_HALO + sub_rows, LANES), F32),
            pltpu.VMEM((POOL_HALO, d), F32),
        ],
        compiler_params=pltpu.CompilerParams(
            dimension_semantics=("arbitrary", "arbitrary"),
            vmem_limit_bytes=_vmem_limit(nbytes)),
        name="pool_mixer",
    )(x, g_in.reshape(1, d), w_in.astype(BF16), w_grp.astype(BF16), scale.reshape(1, d))


def _qkv_kernel(x_ref, gin_ref, w_hbm, *refs, rows, sub_rows, d, n_pairs, layer):
    n_groups = len(DILATED_CFG)
    outs, (slab_ref, perm_ref, w_ref, stage, sem) = refs[:n_groups], refs[n_groups:]
    _load_weights_bf16([(w_hbm.at[layer], w_ref, stage, sem)])
    width = 3 * n_pairs * LANES
    n_slabs = d // LANES
    for t in range(rows // sub_rows):
        xn = _rms(x_ref[0, t * sub_rows:(t + 1) * sub_rows, :], gin_ref[...])
        slab = slab_ref.at[t % 2]
        for s in range(n_slabs):
            slab[s] = xn[:, s * LANES:(s + 1) * LANES]
        for g, (_, dilation) in enumerate(DILATED_CFG):
            sub = sub_rows // dilation
            if dilation == 1:
                xg = xn.astype(BF16)
            else:
                perm = perm_ref.at[t % 2, g % 2]
                for r in range(dilation):
                    for s in range(n_slabs):
                        perm[r * sub:(r + 1) * sub, s * LANES:(s + 1) * LANES] = (
                            slab[s, pl.ds(r, sub, stride=dilation), :].astype(BF16))
                xg = perm[...]
            y = jnp.dot(xg, w_ref[:, g * width:(g + 1) * width], preferred_element_type=F32)
            for i in range(3):
                for hp in range(n_pairs):
                    lo = (i * n_pairs + hp) * LANES
                    part = y[:, lo:lo + LANES]
                    if i == 0:
                        part = part * Q_SCALE
                    for r in range(dilation):
                        outs[g][0, i, hp, r, t * sub:(t + 1) * sub, :] = (
                            part[r * sub:(r + 1) * sub, :].astype(BF16))


def _qkv_proj(x, g_in, w_qkv_stack, layer, n_pairs):
    b, s, d = x.shape
    w_shape = w_qkv_stack.shape[1:]
    rows, sub_rows = BIG_ROW_TILE, ROW_TILE
    out_shapes, out_specs = [], []
    for _, dilation in DILATED_CFG:
        length = s // dilation
        out_shapes.append(jax.ShapeDtypeStruct((b, 3, n_pairs, dilation, length, LANES), BF16))
        out_specs.append(pl.BlockSpec((1, 3, n_pairs, dilation, rows // dilation, LANES),
                                      lambda bi, j: (bi, 0, 0, 0, j, 0)))
    nbytes = (math.prod(w_shape) * 2 + 2 * rows * d * 4 + 2 * rows * w_shape[1] * 2
              + 2 * sub_rows * d * 4 + 4 * sub_rows * d * 2 + WEIGHT_SLOTS * WEIGHT_CHUNK_BYTES)
    outs = pl.pallas_call(
        functools.partial(_qkv_kernel, rows=rows, sub_rows=sub_rows, d=d, n_pairs=n_pairs, layer=layer),
        out_shape=out_shapes,
        grid=(b, s // rows),
        in_specs=[_row_tiled(rows, d), _resident((1, d)), _IN_HBM],
        out_specs=out_specs,
        scratch_shapes=[
            pltpu.VMEM((2, d // LANES, sub_rows, LANES), F32),
            pltpu.VMEM((2, 2, sub_rows, d), BF16),
            pltpu.VMEM(w_shape, BF16),
            *_weight_staging(w_shape[1], w_shape[0]),
        ],
        compiler_params=pltpu.CompilerParams(
            dimension_semantics=("arbitrary", "arbitrary"),
            vmem_limit_bytes=_vmem_limit(nbytes)),
        name="qkv_proj",
    )(x, g_in.reshape(1, d), w_qkv_stack)
    return [o.reshape(b, 3, n_pairs, s, LANES) for o in outs]


def _attn_bias_table(n_pairs):
    n_heads = len(DILATED_CFG) * HEADS_PER_GROUP
    slopes = (2.0 ** (-8.0 * np.arange(1, n_heads + 1) / n_heads)).astype(np.float32)
    delta = ATTN_BLOCK + np.arange(ATTN_BLOCK)[:, None] - np.arange(2 * ATTN_BLOCK)[None, :]
    neg = np.float32(NEG_INF)
    tables = []
    for g, (window, dilation) in enumerate(DILATED_CFG):
        span = window // dilation
        valid = (delta >= 0) & (delta <= span)
        dist = (delta * dilation).astype(np.float32)
        with_prev = np.stack([np.where(valid, -slopes[g * HEADS_PER_GROUP + h] * dist * np.float32(LOG2_E), neg)
                              for h in range(HEADS_PER_GROUP)])
        with_prev = with_prev.reshape(n_pairs, 2, ATTN_BLOCK, 2 * ATTN_BLOCK)
        no_prev = with_prev.copy()
        no_prev[..., :ATTN_BLOCK] = neg
        row_zero = np.full_like(with_prev, neg)
        row_zero[..., :ATTN_BLOCK] = with_prev[..., ATTN_BLOCK:]
        tables += [with_prev, no_prev, row_zero]
    return jnp.asarray(np.stack(tables), F32)


WITH_PREV, NO_PREV, ROW_ZERO = 0, 1, 2
N_BIAS_KINDS = 3


def _attn_kernel(q0, k0, v0, q1, k1, v1, q2, k2, v2, bias_ref, o_ref, acc_ref, m_ref, l_ref, *, seq):
    blk = ATTN_BLOCK
    low_half = lax.broadcasted_iota(jnp.int32, (blk, LANES), 1) < HEAD_DIM

    def scores(kind, q_ref, k_ref, v_ref, q_start, k_start):
        q = q_ref[pl.ds(_aligned(q_start, blk), blk), :]
        k = k_ref[pl.ds(_aligned(k_start, blk), 2 * blk), :]
        v = v_ref[pl.ds(_aligned(k_start, blk), 2 * blk), :]
        v_ext = jnp.concatenate([v, jnp.ones((2 * blk, LANES), BF16)], axis=1)
        m_h, l_h, pv_h = [], [], []
        for a in range(2):
            qa = jnp.where(low_half if a == 0 else ~low_half, q, jnp.zeros_like(q))
            s = lax.dot_general(qa, k, (((1,), (1,)), ((), ())), preferred_element_type=F32)
            s = s + bias_ref[kind, a]
            m = jnp.max(s, axis=-1, keepdims=True)
            p = jnp.exp2(s - m).astype(BF16)
            pvl = jnp.dot(p, v_ext, preferred_element_type=F32)
            m_h.append(m)
            pv_h.append(pvl[:, :LANES])
            l_h.append(pvl[:, LANES:])
        return (jnp.where(low_half, m_h[0], m_h[1]), jnp.where(low_half, l_h[0], l_h[1]),
                jnp.where(low_half, pv_h[0], pv_h[1]))

    def merge(out_rows, first_group, m_b, l_b, pv_b):
        if first_group:
            m_ref[out_rows, :] = m_b
            l_ref[out_rows, :] = l_b
            acc_ref[out_rows, :] = pv_b
        else:
            m_old = m_ref[out_rows, :]
            m_new = jnp.maximum(m_old, m_b)
            a_old = jnp.exp2(m_old - m_new)
            a_blk = jnp.exp2(m_b - m_new)
            m_ref[out_rows, :] = m_new
            l_ref[out_rows, :] = l_ref[out_rows, :] * a_old + l_b * a_blk
            acc_ref[out_rows, :] = acc_ref[out_rows, :] * a_old + pv_b * a_blk

    groups = ((q0, k0, v0), (q1, k1, v1), (q2, k2, v2))
    total_blocks = seq // blk
    order = sorted(range(len(DILATED_CFG)), key=lambda g: -DILATED_CFG[g][1])
    for g in order:
        dilation, refs, first_group = DILATED_CFG[g][1], groups[g], g == order[0]
        n_blocks = total_blocks // dilation
        assert n_blocks & (n_blocks - 1) == 0 and total_blocks % ATTN_UNROLL == 0

        def step(it, carry, g=g, dilation=dilation, refs=refs, first_group=first_group, n_blocks=n_blocks):
            work = []
            for j in range(ATTN_UNROLL):
                f = it * ATTN_UNROLL + j
                r = lax.shift_right_logical(f, n_blocks.bit_length() - 1)
                n = lax.bitwise_and(f, n_blocks - 1)
                kind = jnp.where(n == 0, jnp.where(f == 0, ROW_ZERO, NO_PREV), WITH_PREV)
                start = n * (blk * dilation) + r
                rows = (pl.ds(_aligned(start, blk), blk) if dilation == 1
                        else pl.ds(start, blk, stride=dilation))
                work.append((scores(g * N_BIAS_KINDS + kind, *refs, f * blk, jnp.maximum(f - 1, 0) * blk),
                             rows))
            for vals, rows in work:
                merge(rows, first_group, *vals)
            return carry

        lax.fori_loop(0, total_blocks // ATTN_UNROLL, step, 0)

    o_ref[0] = (acc_ref[...] / l_ref[...]).astype(BF16)


def _attn_core(qkv_groups, bias):
    b, _, n_pairs, s, _ = qkv_groups[0].shape
    in_specs, args = [], []
    for qkv in qkv_groups:
        for i in range(3):
            in_specs.append(pl.BlockSpec((None, None, None, s, LANES),
                                         lambda bi, hp, i=i: (bi, i, hp, 0, 0)))
            args.append(qkv)
    n_kinds = bias.shape[0]
    in_specs.append(pl.BlockSpec((n_kinds, None, 2, ATTN_BLOCK, 2 * ATTN_BLOCK),
                                 lambda bi, hp: (0, hp, 0, 0, 0)))
    nbytes = (2 * 9 * s * LANES * 2 + 2 * n_kinds * 2 * ATTN_BLOCK * 2 * ATTN_BLOCK * 4
              + 2 * s * LANES * 2 + 3 * s * LANES * 4)
    return pl.pallas_call(
        functools.partial(_attn_kernel, seq=s),
        out_shape=jax.ShapeDtypeStruct((b, s, n_pairs * LANES), BF16),
        grid=(b, n_pairs),
        in_specs=in_specs,
        out_specs=pl.BlockSpec((1, s, LANES), lambda bi, hp: (bi, 0, hp)),
        scratch_shapes=[pltpu.VMEM((s, LANES), F32)] * 3,
        compiler_params=pltpu.CompilerParams(
            dimension_semantics=("arbitrary", "arbitrary"),
            vmem_limit_bytes=_vmem_limit(nbytes)),
        name="dilated_attn",
    )(*args, bias)


def _attn_mixer(x, g_in, w_qkv_stack, layer):
    n_pairs = w_qkv_stack.shape[2] // (len(DILATED_CFG) * 3 * LANES)
    assert n_pairs * 2 == HEADS_PER_GROUP and 2 * HEAD_DIM == LANES
    qkv_groups = _qkv_proj(x, g_in, w_qkv_stack, layer, n_pairs)
    return _attn_core(qkv_groups, _attn_bias_table(n_pairs))


def kernel(x, norm_g, attn_w_qkv, attn_w_o, conv_w_in, conv_w_dw, conv_w_out,
           pool_w_in, pool_w_grp, pool_scale, pool_w_out, ffn_w_up, ffn_w_dw, ffn_w_down):
    depth = norm_g.shape[0]
    ia = ib = ic = 0
    for i in range(depth):
        g = norm_g[i]
        kind = i % 3
        if kind == 0:
            y, w_mix, mix_layer = _attn_mixer(x, g[0], attn_w_qkv, ia), attn_w_o, ia
            ia += 1
        elif kind == 1:
            y, w_mix, mix_layer = _conv_mixer(x, g[0], conv_w_in, ib, conv_w_dw[ib]), conv_w_out, ib
            ib += 1
        else:
            y = _pool_mixer(x, g[0], pool_w_in[ic], pool_w_grp[ic], pool_scale[ic])
            w_mix, mix_layer = pool_w_out, ic
            ic += 1
        x = _layer_tail(x, y, w_mix, mix_layer, g[1], g[2], g[3], ffn_w_up, ffn_w_dw[i], ffn_w_down, i)
    return x
```
